```python
import math
import jax
import jax.numpy as jnp
from jax import lax
import numpy as np

D_MODEL = 2048
BATCH = 16
SEQ = 2048
DEPTH = 4

CTX_LEN = 256
GRID_W = 64
RMS_EPS = 1e-6
L2_EPS = 1e-6
ROPE_THETA = 10000.0

CONV_CH = 1024
CONV_K = 3
GDN_HEADS = 8
GDN_DK = 128
GDN_DV = 128
GDN_CONV_K = 3
GDN_CHUNK = 64
MLA_HEADS = 8
MLA_Q_RANK = 768
MLA_KV_RANK = 512
MLA_NOPE = 128
MLA_ROPE = 64
MLA_V = 128
MLA_SCALE = (MLA_NOPE + MLA_ROPE) ** -0.5
ATTN_Q_BLOCK = 128
N_EXPERTS = 16
EXPERT_FF = 1024
EC_CAPACITY = 2

IN_LAYOUT = (
    ('conv_x', CONV_CH), ('conv_b', CONV_CH), ('conv_c', CONV_CH),
    ('gdn_q', GDN_HEADS * GDN_DK), ('gdn_k', GDN_HEADS * GDN_DK),
    ('gdn_v', GDN_HEADS * GDN_DV), ('gdn_z', GDN_HEADS * GDN_DV),
    ('gdn_alpha_f', GDN_HEADS), ('gdn_alpha_b', GDN_HEADS),
    ('gdn_beta_f', GDN_HEADS), ('gdn_beta_b', GDN_HEADS),
    ('mla_cq', MLA_Q_RANK), ('mla_ckv', MLA_KV_RANK), ('mla_kpe', MLA_ROPE),
    ('gate_conv', D_MODEL), ('gate_gdn', D_MODEL), ('gate_mla', D_MODEL),
)
IN_COLS = (3 * CONV_CH + 2 * GDN_HEADS * GDN_DK + 2 * GDN_HEADS * GDN_DV + 4 * GDN_HEADS
           + MLA_Q_RANK + MLA_KV_RANK + MLA_ROPE + 3 * D_MODEL)

kernel_name = 'hybrid_flow_conv_gdn_mla_ecmoe'


def rms_norm(x, g):
    xf = x.astype(jnp.float32)
    y = xf * lax.rsqrt(jnp.mean(xf * xf, axis=-1, keepdims=True) + RMS_EPS)
    return (y * g.astype(jnp.float32)).astype(x.dtype)


def modulate(h, shift, scale):
    return h * (1 + scale) + shift


def l2_normalize(x):
    return x * lax.rsqrt(jnp.sum(x * x, axis=-1, keepdims=True) + L2_EPS)


def split_in(u):
    sizes = [size for _, size in IN_LAYOUT]
    parts = jnp.split(u, np.cumsum(sizes)[:-1].tolist(), axis=-1)
    return {name: part for (name, _), part in zip(IN_LAYOUT, parts)}


def dwconv_centred(x, w):
    ch = x.shape[-1]
    return lax.conv_general_dilated(x, w[:, None, :].astype(x.dtype), window_strides=(1,), padding='SAME',
                                    dimension_numbers=('NWC', 'WIO', 'NWC'), feature_group_count=ch)


def axial_rope_tables(n_tokens):
    rows = n_tokens // GRID_W
    row = jnp.repeat(jnp.arange(rows), GRID_W)
    col = jnp.tile(jnp.arange(GRID_W), rows)
    n_freq = MLA_ROPE // 4
    inv_freq = 1.0 / (ROPE_THETA ** (jnp.arange(n_freq, dtype=jnp.float32) / n_freq))
    ang = jnp.concatenate([row[:, None] * inv_freq, col[:, None] * inv_freq], axis=-1)
    return jnp.cos(ang), jnp.sin(ang)


def apply_rope(x, cos, sin):
    x1, x2 = jnp.split(x, 2, axis=-1)
    return jnp.concatenate([x1 * cos - x2 * sin, x1 * sin + x2 * cos], axis=-1)


def short_conv_mixer(p, lw):
    y = p['conv_b'] * dwconv_centred(p['conv_c'] * p['conv_x'], lw['conv_w'])
    return y @ lw['conv_out']


def gdn_prepare(p, lw):
    qkv = jnp.concatenate([p['gdn_q'], p['gdn_k'], p['gdn_v']], axis=-1)
    qkv = jax.nn.silu(dwconv_centred(qkv, lw['gdn_conv_w'])).astype(jnp.float32)
    b, t, _ = qkv.shape
    q, k, v = jnp.split(qkv, [GDN_HEADS * GDN_DK, 2 * GDN_HEADS * GDN_DK], axis=-1)
    q = l2_normalize(q.reshape(b, t, GDN_HEADS, GDN_DK)) * (GDN_DK ** -0.5)
    k = l2_normalize(k.reshape(b, t, GDN_HEADS, GDN_DK))
    v = v.reshape(b, t, GDN_HEADS, GDN_DV)
    alpha = jnp.stack([p['gdn_alpha_f'], p['gdn_alpha_b']]).astype(jnp.float32)
    beta_logit = jnp.stack([p['gdn_beta_f'], p['gdn_beta_b']]).astype(jnp.float32)
    a_log = lw['gdn_a_log'].astype(jnp.float32)[:, None, None, :]
    dt_bias = lw['gdn_dt_bias'].astype(jnp.float32)[:, None, None, :]
    g = -jnp.exp(a_log) * jax.nn.softplus(alpha + dt_bias)
    beta = jax.nn.sigmoid(beta_logit)
    return q, k, v, g, beta


def gated_delta_chunked(q, k, v, g, beta, s0):
    b, t, h, dk = q.shape
    dv = v.shape[-1]
    lc = GDN_CHUNK
    nc = t // lc

    def chunk(a):
        a = a.reshape((b, nc, lc, h) + a.shape[3:])
        return jnp.moveaxis(a, (1, 3), (0, 2))

    qc, kc, vc, gc, bc = chunk(q), chunk(k), chunk(v), chunk(g), chunk(beta)
    gcum = jnp.cumsum(gc, axis=-1)
    pos = jnp.arange(lc)
    incl = pos[:, None] >= pos[None, :]
    strict = pos[:, None] > pos[None, :]
    diff = gcum[..., :, None] - gcum[..., None, :]
    decay = jnp.where(incl, jnp.exp(jnp.where(incl, diff, 0.0)), 0.0)
    kb = kc * bc[..., None]
    eye = jnp.eye(lc, dtype=jnp.float32)
    a_mat = jnp.where(strict, jnp.einsum('nbhid,nbhjd->nbhij', kb, kc) * decay, 0.0) + eye
    t_inv = lax.linalg.triangular_solve(a_mat, jnp.broadcast_to(eye, a_mat.shape), left_side=True,
                                        lower=True, unit_diagonal=True)
    u = jnp.einsum('nbhij,nbhjd->nbhid', t_inv, vc * bc[..., None])
    w = jnp.einsum('nbhij,nbhjd->nbhid', t_inv, kb * jnp.exp(gcum)[..., None])
    intra = jnp.einsum('nbhid,nbhjd->nbhij', qc, kc) * decay

    def step(state, xs):
        q_i, k_i, u_i, w_i, g_i, a_i = xs
        v_new = u_i - jnp.einsum('bhld,bhde->bhle', w_i, state)
        o_i = (jnp.einsum('bhld,bhde->bhle', q_i * jnp.exp(g_i)[..., None], state)
               + jnp.einsum('bhij,bhje->bhie', a_i, v_new))
        g_last = g_i[..., -1:]
        k_dec = k_i * jnp.exp(g_last - g_i)[..., None]
        state = state * jnp.exp(g_last)[..., None] + jnp.einsum('bhld,bhle->bhde', k_dec, v_new)
        return state, o_i

    state, o = lax.scan(step, s0, (qc, kc, u, w, gcum, intra))
    o = jnp.moveaxis(o, (0, 2), (1, 3)).reshape(b, t, h, dv)
    return o, state


def gdn_bidirectional(prep, s0_f, s0_b):
    q, k, v, g, beta = prep
    o_f, s_f = gated_delta_chunked(q, k, v, g[0], beta[0], s0_f)
    flip = lambda a: jnp.flip(a, axis=1)
    o_b, s_b = gated_delta_chunked(flip(q), flip(k), flip(v), flip(g[1]), flip(beta[1]), s0_b)
    return o_f + flip(o_b), s_f, s_b


def gdn_output(o, z, lw):
    b, t = z.shape[:2]
    zf = z.astype(jnp.float32).reshape(b, t, GDN_HEADS, GDN_DV)
    y = rms_norm(o, lw['gdn_norm_g']) * jax.nn.silu(zf)
    return y.reshape(b, t, GDN_HEADS * GDN_DV).astype(z.dtype) @ lw['gdn_out']


def mla_queries(p, lw, rope):
    b, t, _ = p['mla_cq'].shape
    q = (rms_norm(p['mla_cq'], lw['mla_q_norm_g']) @ lw['mla_w_uq']).reshape(b, t, MLA_HEADS, MLA_NOPE + MLA_ROPE)
    if rope is not None:
        cos, sin = rope
        q = jnp.concatenate([q[..., :MLA_NOPE], apply_rope(q[..., MLA_NOPE:], cos[:, None, :], sin[:, None, :])], axis=-1)
    return q


def mla_keys_values(p, lw, rope):
    b, t, _ = p['mla_ckv'].shape
    kv = (rms_norm(p['mla_ckv'], lw['mla_kv_norm_g']) @ lw['mla_w_ukv']).reshape(b, t, MLA_HEADS, MLA_NOPE + MLA_V)
    k_pe = p['mla_kpe']
    if rope is not None:
        cos, sin = rope
        k_pe = apply_rope(k_pe, cos, sin)
    k = jnp.concatenate([kv[..., :MLA_NOPE], jnp.broadcast_to(k_pe[:, :, None, :], (b, t, MLA_HEADS, MLA_ROPE))], axis=-1)
    return k, kv[..., MLA_NOPE:]


def softmax_attention(q, k, v):
    s = jnp.einsum('bqhd,bkhd->bhqk', q, k, preferred_element_type=jnp.float32) * MLA_SCALE
    prob = jax.nn.softmax(s, axis=-1).astype(v.dtype)
    return jnp.einsum('bhqk,bkhd->bqhd', prob, v)


def blocked_attention(q, k, v):
    b, t, h, d = q.shape
    nb = t // ATTN_Q_BLOCK
    qb = jnp.moveaxis(q.reshape(b, nb, ATTN_Q_BLOCK, h, d), 1, 0)
    ob = lax.map(lambda q_blk: softmax_attention(q_blk, k, v), qb)
    return jnp.moveaxis(ob, 0, 1).reshape(b, t, h, v.shape[-1])


def merge_branches(p, lw, o_gdn, att):
    b, t = att.shape[:2]
    y_conv = short_conv_mixer(p, lw)
    y_gdn = gdn_output(o_gdn, p['gdn_z'], lw)
    y_mla = att.reshape(b, t, MLA_HEADS * MLA_V) @ lw['mla_out']
    m = (jax.nn.sigmoid(p['gate_conv']) * y_conv + jax.nn.sigmoid(p['gate_gdn']) * y_gdn
         + jax.nn.sigmoid(p['gate_mla']) * y_mla)
    return m @ lw['w_o']


def hybrid_mixer(a_lat, a_ctx, lw, rope, need_ctx):
    p_lat = split_in(a_lat @ lw['w_in'])
    p_ctx = split_in(a_ctx @ lw['w_in'])
    s0 = jnp.zeros((a_ctx.shape[0], GDN_HEADS, GDN_DK, GDN_DV), jnp.float32)
    o_ctx, s_f, s_b = gdn_bidirectional(gdn_prepare(p_ctx, lw), s0, s0)
    o_lat, _, _ = gdn_bidirectional(gdn_prepare(p_lat, lw), s_f, s_b)
    k_ctx, v_ctx = mla_keys_values(p_ctx, lw, None)
    k_lat, v_lat = mla_keys_values(p_lat, lw, rope)
    att_lat = blocked_attention(mla_queries(p_lat, lw, rope), jnp.concatenate([k_ctx, k_lat], axis=1),
                                jnp.concatenate([v_ctx, v_lat], axis=1))
    y_lat = merge_branches(p_lat, lw, o_lat, att_lat)
    if not need_ctx:
        return y_lat, None
    att_ctx = softmax_attention(mla_queries(p_ctx, lw, None), k_ctx, v_ctx)
    y_ctx = merge_branches(p_ctx, lw, o_ctx, att_ctx)
    return y_lat, y_ctx


def expert_choice_ffn(h, router, w_gate, w_up, w_down):
    b, t, d = h.shape
    cap = EC_CAPACITY * t // N_EXPERTS
    aff = jax.nn.softmax(jnp.einsum('btd,de->bte', h, router, preferred_element_type=jnp.float32), axis=-1)
    gate, idx = lax.top_k(jnp.swapaxes(aff, 1, 2), cap)
    xs = jax.vmap(lambda hb, ib: hb[ib])(h, idx)
    hid = jax.nn.silu(jnp.einsum('becd,edf->becf', xs, w_gate)) * jnp.einsum('becd,edf->becf', xs, w_up)
    ye = jnp.einsum('becf,efd->becd', hid, w_down) * gate[..., None].astype(h.dtype)
    return jax.vmap(lambda yb, ib: jnp.zeros((t, d), h.dtype).at[ib.reshape(-1)].add(yb.reshape(-1, d)))(ye, idx)


def setup_inputs(seed: int = 0) -> dict:
    key = jax.random.key(seed)
    ks = iter(jax.random.split(key, 32))
    f32 = jnp.float32
    L, D, H = DEPTH, D_MODEL, GDN_HEADS

    def dense(shape, fan_in):
        return jax.random.normal(next(ks), shape, f32) * (fan_in ** -0.5)

    def gain(shape):
        return 1.0 + 0.02 * jax.random.normal(next(ks), shape, f32)

    x = jax.random.normal(next(ks), (BATCH, SEQ, D), f32)
    c = jax.random.normal(next(ks), (BATCH, D), f32)
    ctx = jax.random.normal(next(ks), (BATCH, CTX_LEN, D), f32)
    c_ctx = jax.random.normal(next(ks), (D,), f32)
    w_ada = dense((L, D, 6 * D), D)
    b_ada = 0.02 * jax.random.normal(next(ks), (L, 6 * D), f32)
    norm1_g = gain((L, D))
    w_in = dense((L, D, IN_COLS), D)
    conv_w = dense((L, CONV_K, CONV_CH), CONV_K)
    conv_out = dense((L, CONV_CH, D), CONV_CH)
    gdn_conv_w = dense((L, GDN_CONV_K, 2 * H * GDN_DK + H * GDN_DV), GDN_CONV_K)
    gdn_a_log = jnp.log(jax.random.uniform(next(ks), (L, 2, H), f32, 1.0, 16.0))
    dt = jnp.exp(jax.random.uniform(next(ks), (L, 2, H), f32, math.log(1e-3), math.log(1e-1)))
    gdn_dt_bias = dt + jnp.log(-jnp.expm1(-dt))
    gdn_norm_g = gain((L, GDN_DV))
    gdn_out = dense((L, H * GDN_DV, D), H * GDN_DV)
    mla_q_norm_g = gain((L, MLA_Q_RANK))
    mla_w_uq = dense((L, MLA_Q_RANK, MLA_HEADS * (MLA_NOPE + MLA_ROPE)), MLA_Q_RANK)
    mla_kv_norm_g = gain((L, MLA_KV_RANK))
    mla_w_ukv = dense((L, MLA_KV_RANK, MLA_HEADS * (MLA_NOPE + MLA_V)), MLA_KV_RANK)
    mla_out = dense((L, MLA_HEADS * MLA_V, D), MLA_HEADS * MLA_V)
    w_o = dense((L, D, D), D)
    norm2_g = gain((L, D))
    router = dense((L, D, N_EXPERTS), D)
    w_gate = dense((L, N_EXPERTS, D, EXPERT_FF), D)
    w_up = dense((L, N_EXPERTS, D, EXPERT_FF), D)
    w_down = dense((L, N_EXPERTS, EXPERT_FF, D), EXPERT_FF)
    final_g = gain((D,))
    return {'x': x, 'c': c, 'ctx': ctx, 'c_ctx': c_ctx, 'w_ada': w_ada, 'b_ada': b_ada, 'norm1_g': norm1_g,
            'w_in': w_in, 'conv_w': conv_w, 'conv_out': conv_out, 'gdn_conv_w': gdn_conv_w,
            'gdn_a_log': gdn_a_log, 'gdn_dt_bias': gdn_dt_bias, 'gdn_norm_g': gdn_norm_g, 'gdn_out': gdn_out,
            'mla_q_norm_g': mla_q_norm_g, 'mla_w_uq': mla_w_uq, 'mla_kv_norm_g': mla_kv_norm_g,
            'mla_w_ukv': mla_w_ukv, 'mla_out': mla_out, 'w_o': w_o, 'norm2_g': norm2_g, 'router': router,
            'w_gate': w_gate, 'w_up': w_up, 'w_down': w_down, 'final_g': final_g}


def reference(x, c, ctx, c_ctx, w_ada, b_ada, norm1_g, w_in, conv_w, conv_out, gdn_conv_w, gdn_a_log, gdn_dt_bias,
              gdn_norm_g, gdn_out, mla_q_norm_g, mla_w_uq, mla_kv_norm_g, mla_w_ukv, mla_out, w_o, norm2_g, router,
              w_gate, w_up, w_down, final_g):
    cos, sin = axial_rope_tables(x.shape[1])
    rope = (cos.astype(x.dtype), sin.astype(x.dtype))
    h, hc = x, ctx
    for l in range(DEPTH):
        need_ctx = l < DEPTH - 1
        lw = {'w_in': w_in[l], 'conv_w': conv_w[l], 'conv_out': conv_out[l], 'gdn_conv_w': gdn_conv_w[l],
              'gdn_a_log': gdn_a_log[l], 'gdn_dt_bias': gdn_dt_bias[l], 'gdn_norm_g': gdn_norm_g[l],
              'gdn_out': gdn_out[l], 'mla_q_norm_g': mla_q_norm_g[l], 'mla_w_uq': mla_w_uq[l],
              'mla_kv_norm_g': mla_kv_norm_g[l], 'mla_w_ukv': mla_w_ukv[l], 'mla_out': mla_out[l], 'w_o': w_o[l]}
        mod = jax.nn.silu(c)[:, None, :] @ w_ada[l] + b_ada[l]
        mod_c = (jax.nn.silu(c_ctx) @ w_ada[l] + b_ada[l])[None, None, :]
        sh1, sc1, g1, sh2, sc2, g2 = jnp.split(mod, 6, axis=-1)
        csh1, csc1, cg1, csh2, csc2, cg2 = jnp.split(mod_c, 6, axis=-1)
        a_lat = modulate(rms_norm(h, norm1_g[l]), sh1, sc1)
        a_ctx = modulate(rms_norm(hc, norm1_g[l]), csh1, csc1)
        y_lat, y_ctx = hybrid_mixer(a_lat, a_ctx, lw, rope, need_ctx)
        h = h + g1 * y_lat
        h = h + g2 * expert_choice_ffn(modulate(rms_norm(h, norm2_g[l]), sh2, sc2), router[l], w_gate[l],
                                       w_up[l], w_down[l])
        if need_ctx:
            hc = hc + cg1 * y_ctx
            hc = hc + cg2 * expert_choice_ffn(modulate(rms_norm(hc, norm2_g[l]), csh2, csc2), router[l],
                                              w_gate[l], w_up[l], w_down[l])
    return rms_norm(h, final_g)
```

```python
import functools
import math

import jax
import jax.numpy as jnp
from jax import lax
from jax.experimental import pallas as pl
from jax.experimental.pallas import tpu as pltpu

F32 = jnp.float32
BF16 = jnp.bfloat16

RMS_EPS = 1e-6
L2_EPS = 1e-6
ROPE_THETA = 10000.0
GRID_W = 64
HEAD_DIM = 128
ROPE_DIM = 64
QK_DIM = HEAD_DIM + ROPE_DIM
GDN_CHUNK = 64
GDN_SUB = 16
EC_CAPACITY = 2
LANE = 128
VMEM_LIMIT = 56 * 1024 * 1024


def _pick(n, cands):
    ns = n if isinstance(n, tuple) else (n,)
    for c in cands:
        if all(v % c == 0 for v in ns):
            return c
    raise ValueError(f"no tile in {cands} divides {ns}")


def _params(*sem):
    return pltpu.CompilerParams(dimension_semantics=sem, vmem_limit_bytes=VMEM_LIMIT)


def _dot(a, b):
    return jnp.dot(a, b, preferred_element_type=F32)


def _dot_nt(a, b):
    return lax.dot_general(a, b, (((1,), (1,)), ((), ())), preferred_element_type=F32)


def _dot_tn(a, b):
    return lax.dot_general(a, b, (((0,), (0,)), ((), ())), preferred_element_type=F32)


def _split2(x):
    hi = x.astype(BF16)
    lo = (x - hi.astype(F32)).astype(BF16)
    return hi, lo


def _split3(x):
    hi = x.astype(BF16)
    r = x - hi.astype(F32)
    mid = r.astype(BF16)
    lo = (r - mid.astype(F32)).astype(BF16)
    return hi, mid, lo


def _mm3(a, b):
    ah, al = _split2(a)
    bh, bl = _split2(b)
    return _dot(ah, bh) + (_dot(ah, bl) + _dot(al, bh))


def _silu(x):
    return x * jax.nn.sigmoid(x)


def _softplus(x):
    return jnp.maximum(x, 0.0) + jnp.log(1.0 + jnp.exp(-jnp.abs(x)))


def _ada_kernel(c_ref, w_ref, b_ref, o_ref):
    a = _silu(c_ref[...])
    ah, al = _split2(a)
    wh, wl = _split2(w_ref[0])
    o_ref[0] = _dot(ah, wh) + (_dot(ah, wl) + _dot(al, wh)) + b_ref[0]


def ada_modulation(cc, w_ada, b_ada):
    L, D, D6 = w_ada.shape
    R = cc.shape[0]
    tn = _pick(D6, (1024, 512, 256, 128))
    return pl.pallas_call(
        _ada_kernel,
        grid=(L, D6 // tn),
        in_specs=[pl.BlockSpec((R, D), lambda l, j: (0, 0)),
                  pl.BlockSpec((1, D, tn), lambda l, j: (l, 0, j)),
                  pl.BlockSpec((1, 1, tn), lambda l, j: (l, 0, j))],
        out_specs=pl.BlockSpec((1, R, tn), lambda l, j: (l, 0, j)),
        out_shape=jax.ShapeDtypeStruct((L, R, D6), F32),
        compiler_params=_params("parallel", "parallel"),
        name="ada_modulation",
    )(cc, w_ada, b_ada.reshape(L, 1, D6))


def _norm_mod(x, g, sh, sc):
    ms = jnp.mean(x * x, axis=-1, keepdims=True)
    y = x * lax.rsqrt(ms + RMS_EPS) * g
    return y * (1.0 + sc) + sh


def _in_proj_kernel(x_ref, g_ref, sh_ref, sc_ref, w_ref, o_ref, a_scr):
    @pl.when(pl.program_id(1) == 0)
    def _():
        a_scr[...] = _norm_mod(x_ref[...], g_ref[...], sh_ref[0], sc_ref[0]).astype(BF16)

    o_ref[...] = _dot(a_scr[...], w_ref[...]).astype(o_ref.dtype)


def in_projection(x, g, sh, sc, w, rows_per_group):
    N, D = x.shape
    NC = w.shape[1]
    tm = _pick(rows_per_group, (1024, 512, 256, 128))
    tn = _pick(NC, (512, 256, 128))
    gdiv = rows_per_group // tm
    return pl.pallas_call(
        _in_proj_kernel,
        grid=(N // tm, NC // tn),
        in_specs=[pl.BlockSpec((tm, D), lambda i, j: (i, 0)),
                  pl.BlockSpec((1, D), lambda i, j: (0, 0)),
                  pl.BlockSpec((1, 1, D), lambda i, j: (i // gdiv, 0, 0)),
                  pl.BlockSpec((1, 1, D), lambda i, j: (i // gdiv, 0, 0)),
                  pl.BlockSpec((D, tn), lambda i, j: (0, j))],
        out_specs=pl.BlockSpec((tm, tn), lambda i, j: (i, j)),
        out_shape=jax.ShapeDtypeStruct((N, NC), BF16),
        scratch_shapes=[pltpu.VMEM((tm, D), BF16)],
        compiler_params=_params("parallel", "arbitrary"),
        name="in_projection",
    )(x, g.reshape(1, D), sh, sc, w)


def _conv3(x, w, T):
    row = lax.broadcasted_iota(jnp.int32, x.shape, 0)
    prev = jnp.where(row == 0, 0.0, pltpu.roll(x, 1, 0))
    nxt = jnp.where(row == T - 1, 0.0, pltpu.roll(x, T - 1, 0))
    return prev * w[0:1] + x * w[1:2] + nxt * w[2:3]


def _conv_mixer_kernel(x_ref, b_ref, c_ref, w_ref, o_ref, *, T):
    v = c_ref[...].astype(F32) * x_ref[...].astype(F32)
    o_ref[...] = (b_ref[...].astype(F32) * _conv3(v, w_ref[...], T)).astype(o_ref.dtype)


def conv_mixer_front(u, conv_w, B, T, off):
    C = conv_w.shape[1]
    tc = _pick(C, (256, 128))
    ox, ob, oc = (off[k] // tc for k in ("conv_x", "conv_b", "conv_c"))
    return pl.pallas_call(
        functools.partial(_conv_mixer_kernel, T=T),
        grid=(B, C // tc),
        in_specs=[pl.BlockSpec((T, tc), lambda b, j: (b, ox + j)),
                  pl.BlockSpec((T, tc), lambda b, j: (b, ob + j)),
                  pl.BlockSpec((T, tc), lambda b, j: (b, oc + j)),
                  pl.BlockSpec((3, tc), lambda b, j: (0, j))],
        out_specs=pl.BlockSpec((T, tc), lambda b, j: (b, j)),
        out_shape=jax.ShapeDtypeStruct((B * T, C), BF16),
        compiler_params=_params("parallel", "parallel"),
        name="conv_mixer_front",
    )(u, u, u, conv_w)


def _gdn_inverse(a, sub_mask, eye):
    d = jnp.where(sub_mask, a, 0.0)
    n = a - d
    d2 = _mm3(d, d)
    d4 = _mm3(d2, d2)
    d8 = _mm3(d4, d4)
    x = eye - d
    x = x + _mm3(x, d2)
    x = x + _mm3(x, d4)
    x = x + _mm3(x, d8)
    m = _mm3(x, n)
    m2 = _mm3(m, m)
    y = eye - m
    y = y + _mm3(y, m2)
    return _mm3(y, x)


def _gdn_kernel(q_ref, k_ref, v_ref, z_ref, ab_ref, wq_ref, wk_ref, wv_ref, alog_ref, dtb_ref, ng_ref,
                s0f_ref, s0b_ref, o_ref, sf_ref, sb_ref,
                q_s, k_s, v_s, col_s, u_s, w_s, qg_s, kd_s, in_s, eg_s, of_s, ob_s, *, T, H, G):
    h = pl.program_id(1)
    NC = T // GDN_CHUNK
    L = GDN_CHUNK

    def prep(x_ref, w_ref):
        return _silu(_conv3(x_ref[...].astype(F32), w_ref[...], T))

    def l2n(x):
        return x * lax.rsqrt(jnp.sum(x * x, axis=-1, keepdims=True) + L2_EPS)

    q_s[...] = l2n(prep(q_ref, wq_ref)) * (HEAD_DIM ** -0.5)
    k_s[...] = l2n(prep(k_ref, wk_ref))
    v_s[...] = prep(v_ref, wv_ref)

    ab = ab_ref[...].astype(F32)
    lane = lax.broadcasted_iota(jnp.int32, ab.shape, 1)
    gfull = -jnp.exp(alog_ref[...]) * _softplus(ab + dtb_ref[...])
    bfull = jax.nn.sigmoid(ab)

    def col(x, idx):
        return jnp.sum(jnp.where(lane == idx, x, 0.0), axis=1, keepdims=True)

    g_f, g_b = col(gfull, h), col(gfull, H + h)
    b_f, b_b = col(bfull, 2 * H + h), col(bfull, 3 * H + h)
    col_s[...] = jnp.where(lane == 0, g_f, jnp.where(lane == 1, g_b, jnp.where(lane == 2, b_f, b_b)))

    ri = lax.broadcasted_iota(jnp.int32, (G, G), 0)
    ci = lax.broadcasted_iota(jnp.int32, (G, G), 1)
    same_chunk = (ri >> int(math.log2(L))) == (ci >> int(math.log2(L)))
    same_sub = (ri >> int(math.log2(GDN_SUB))) == (ci >> int(math.log2(GDN_SUB)))
    eye = jnp.where(ri == ci, 1.0, 0.0).astype(F32)
    ones_chunk = jnp.where(same_chunk, 1.0, 0.0).astype(BF16)
    diag_b = eye.astype(BF16)
    ones_gg = jnp.ones((G, G), BF16)

    def exact3(m_bf, x):
        xh, xm, xl = _split3(x)
        return _dot(m_bf, xh) + _dot(m_bf, xm) + _dot(m_bf, xl)

    def group_body(gi, carry):
        r0 = pl.multiple_of(gi * G, G)
        rows = pl.ds(r0, G)
        qg, kg, vg = q_s[rows, :], k_s[rows, :], v_s[rows, :]
        cols = col_s[rows, :]
        kb16 = kg.astype(BF16)
        kk = _dot_nt(kb16, kb16)
        qk = _dot_nt(qg.astype(BF16), kb16)
        for d in range(2):
            incl = same_chunk & ((ci <= ri) if d == 0 else (ci >= ri))
            strict = same_chunk & ((ci < ri) if d == 0 else (ci > ri))
            gcol, bcol = cols[:, d:d + 1], cols[:, 2 + d:3 + d]
            gb = jnp.broadcast_to(gcol, (G, LANE))
            gc = exact3(jnp.where(incl, 1.0, 0.0).astype(BF16), gb)
            gl = exact3(ones_chunk, gb)
            cmat = jnp.concatenate([gc] * (G // LANE), axis=1) if G > LANE else gc
            rmat = exact3(ones_gg, jnp.where(ri == ci, cmat, 0.0))
            decay = jnp.where(incl, jnp.exp(jnp.where(incl, cmat - rmat, 0.0)), 0.0)
            a = jnp.where(strict, bcol * kk * decay, 0.0)
            tinv = _gdn_inverse(a, same_sub, eye)
            rhs = jnp.concatenate([vg * bcol, kg * bcol * jnp.exp(gc)], axis=1).astype(BF16)
            uw = _dot(tinv.astype(BF16), rhs)
            u_s[d, rows, :] = uw[:, :HEAD_DIM]
            w_s[d, rows, :] = uw[:, HEAD_DIM:]
            qg_s[d, rows, :] = qg * jnp.exp(gc)
            kd_s[d, rows, :] = kg * jnp.exp(gl - gc)
            eg_s[d, rows, :] = jnp.exp(gl)
            intra = qk * decay
            for c in range(G // L):
                in_s[d, pl.ds(r0 + c * L, L), :] = intra[c * L:(c + 1) * L, c * L:(c + 1) * L]
        return carry

    lax.fori_loop(0, T // G, group_body, 0)

    def chunk_step(d, r, s, o_scr):
        rows = pl.ds(r, L)
        wq = jnp.concatenate([w_s[d, rows, :], qg_s[d, rows, :]], axis=0).astype(BF16)
        ws = _dot(wq, s.astype(BF16))
        v_new = u_s[d, rows, :] - ws[:L]
        vb = v_new.astype(BF16)
        o_scr[rows, :] = ws[L:] + _dot(in_s[d, rows, :].astype(BF16), vb)
        eg = eg_s[d, pl.ds(r, 8), :][0:1]
        return s * eg + _dot_tn(kd_s[d, rows, :].astype(BF16), vb)

    def scan_body(c, carry):
        sf, sb = carry
        rf = pl.multiple_of(c * L, L)
        rb = pl.multiple_of((NC - 1 - c) * L, L)
        return chunk_step(0, rf, sf, of_s), chunk_step(1, rb, sb, ob_s)

    sf, sb = lax.fori_loop(0, NC, scan_body, (s0f_ref[0, 0], s0b_ref[0, 0]))
    sf_ref[0, 0] = sf
    sb_ref[0, 0] = sb

    o = of_s[...] + ob_s[...]
    y = o * lax.rsqrt(jnp.mean(o * o, axis=-1, keepdims=True) + RMS_EPS) * ng_ref[...]
    o_ref[...] = (y * _silu(z_ref[...].astype(F32))).astype(o_ref.dtype)


def gdn_mixer(u, conv_w, alog_row, dtb_row, norm_g, s0f, s0b, B, T, H, off):
    G = 256 if T % 256 == 0 else 128
    oq, ok, ov, oz, oab = (off[k] // LANE for k in ("gdn_q", "gdn_k", "gdn_v", "gdn_z", "gdn_small"))
    seq = lambda o: pl.BlockSpec((T, LANE), lambda b, h: (b, o + h))
    cw = lambda o: pl.BlockSpec((3, LANE), lambda b, h: (0, o * H + h))
    row = pl.BlockSpec((1, LANE), lambda b, h: (0, 0))
    st = pl.BlockSpec((1, 1, HEAD_DIM, HEAD_DIM), lambda b, h: (b, h, 0, 0))
    tbuf = lambda: pltpu.VMEM((T, HEAD_DIM), F32)
    dbuf = lambda: pltpu.VMEM((2, T, HEAD_DIM), F32)
    return pl.pallas_call(
        functools.partial(_gdn_kernel, T=T, H=H, G=G),
        grid=(B, H),
        in_specs=[seq(oq), seq(ok), seq(ov), seq(oz),
                  pl.BlockSpec((T, LANE), lambda b, h: (b, oab)),
                  cw(0), cw(1), cw(2), row, row, row, st, st],
        out_specs=[pl.BlockSpec((T, HEAD_DIM), lambda b, h: (b, h)), st, st],
        out_shape=[jax.ShapeDtypeStruct((B * T, H * HEAD_DIM), BF16),
                   jax.ShapeDtypeStruct((B, H, HEAD_DIM, HEAD_DIM), F32),
                   jax.ShapeDtypeStruct((B, H, HEAD_DIM, HEAD_DIM), F32)],
        scratch_shapes=[tbuf(), tbuf(), tbuf(), tbuf(), dbuf(), dbuf(), dbuf(), dbuf(),
                        pltpu.VMEM((2, T, GDN_CHUNK), F32), dbuf(), tbuf(), tbuf()],
        compiler_params=_params("parallel", "parallel"),
        name="gdn_mixer",
    )(u, u, u, u, u, conv_w, conv_w, conv_w, alog_row, dtb_row, norm_g.reshape(1, HEAD_DIM), s0f, s0b)


def _rms(x, g):
    return x * lax.rsqrt(jnp.mean(x * x, axis=-1, keepdims=True) + RMS_EPS) * g


def _mla_q_kernel(x_ref, g_ref, wm_ref, wa_ref, ct_ref, st_ref, o_ref, a_scr):
    @pl.when(pl.program_id(1) == 0)
    def _():
        a_scr[...] = _rms(x_ref[...].astype(F32), g_ref[...]).astype(BF16)

    a = a_scr[...]
    o_ref[0, 0] = (_dot(a, wm_ref[0]) * ct_ref[...] + _dot(a, wa_ref[0]) * st_ref[...]).astype(o_ref.dtype)


def mla_queries(u, norm_g, w_main, w_aux, ctab, stab, B, T, off):
    Hm, R, _ = w_main.shape
    tm = _pick(T, (512, 256, 128))
    nt = T // tm
    oc = off["mla_cq"] // R
    return pl.pallas_call(
        _mla_q_kernel,
        grid=(B * nt, Hm),
        in_specs=[pl.BlockSpec((tm, R), lambda i, h: (i, oc)),
                  pl.BlockSpec((1, R), lambda i, h: (0, 0)),
                  pl.BlockSpec((1, R, QK_DIM), lambda i, h: (h, 0, 0)),
                  pl.BlockSpec((1, R, QK_DIM), lambda i, h: (h, 0, 0)),
                  pl.BlockSpec((tm, QK_DIM), lambda i, h: (i % nt, 0)),
                  pl.BlockSpec((tm, QK_DIM), lambda i, h: (i % nt, 0))],
        out_specs=pl.BlockSpec((1, 1, tm, QK_DIM), lambda i, h: (i // nt, h, i % nt, 0)),
        out_shape=jax.ShapeDtypeStruct((B, Hm, T, QK_DIM), BF16),
        scratch_shapes=[pltpu.VMEM((tm, R), BF16)],
        compiler_params=_params("parallel", "arbitrary"),
        name="mla_queries",
    )(u, norm_g.reshape(1, R), w_main, w_aux, ctab, stab)


def _mla_kv_kernel(x_ref, pe_ref, g_ref, wk_ref, wv_ref, tab_ref, k_ref, v_ref, a_scr, pe_scr):
    @pl.when(pl.program_id(1) == 0)
    def _():
        a_scr[...] = _rms(x_ref[...].astype(F32), g_ref[...]).astype(BF16)
        r = pe_ref[...].astype(F32) * tab_ref[...]
        pe_scr[...] = r + pltpu.roll(r, ROPE_DIM, 1)

    a = a_scr[...]
    k_ref[0, 0, :, 0:HEAD_DIM] = _dot(a, wk_ref[0]).astype(k_ref.dtype)
    k_ref[0, 0, :, HEAD_DIM:QK_DIM] = pe_scr[:, 0:ROPE_DIM].astype(k_ref.dtype)
    v_ref[0, 0] = _dot(a, wv_ref[0]).astype(v_ref.dtype)


def mla_keys_values(u, norm_g, w_k, w_v, tab, B, T, off):
    Hm, R, _ = w_k.shape
    tm = _pick(T, (512, 256, 128))
    nt = T // tm
    oc, op = off["mla_ckv"] // R, off["mla_kpe"] // LANE
    return pl.pallas_call(
        _mla_kv_kernel,
        grid=(B * nt, Hm),
        in_specs=[pl.BlockSpec((tm, R), lambda i, h: (i, oc)),
                  pl.BlockSpec((tm, LANE), lambda i, h: (i, op)),
                  pl.BlockSpec((1, R), lambda i, h: (0, 0)),
                  pl.BlockSpec((1, R, HEAD_DIM), lambda i, h: (h, 0, 0)),
                  pl.BlockSpec((1, R, HEAD_DIM), lambda i, h: (h, 0, 0)),
                  pl.BlockSpec((tm, LANE), lambda i, h: (i % nt, 0))],
        out_specs=[pl.BlockSpec((1, 1, tm, QK_DIM), lambda i, h: (i // nt, h, i % nt, 0)),
                   pl.BlockSpec((1, 1, tm, HEAD_DIM), lambda i, h: (i // nt, h, i % nt, 0))],
        out_shape=[jax.ShapeDtypeStruct((B, Hm, T, QK_DIM), BF16),
                   jax.ShapeDtypeStruct((B, Hm, T, HEAD_DIM), BF16)],
        scratch_shapes=[pltpu.VMEM((tm, R), BF16), pltpu.VMEM((tm, LANE), F32)],
        compiler_params=_params("parallel", "arbitrary"),
        name="mla_keys_values",
    )(u, u, norm_g.reshape(1, R), w_k, w_v, tab)


def _attn_kernel(*refs, nseg, scale):
    q_ref = refs[0]
    k_refs = refs[1:1 + nseg]
    v_refs = refs[1 + nseg:1 + 2 * nseg]
    o_ref = refs[1 + 2 * nseg]
    q = q_ref[0, 0]
    s = [_dot_nt(q, k[0, 0]) * scale for k in k_refs]
    m = functools.reduce(jnp.maximum, [jnp.max(x, axis=-1, keepdims=True) for x in s])
    p = [jnp.exp(x - m) for x in s]
    l = functools.reduce(jnp.add, [jnp.sum(x, axis=-1, keepdims=True) for x in p])
    acc = functools.reduce(jnp.add, [_dot(x.astype(BF16), v[0, 0]) for x, v in zip(p, v_refs)])
    o_ref[...] = (acc / l).astype(o_ref.dtype)


def attention(q, ks, vs):
    B, Hm, Tq, _ = q.shape
    tq = _pick(Tq, (512, 256, 128))
    nq = Tq // tq
    nseg = len(ks)
    kspec = lambda a: pl.BlockSpec((1, 1) + a.shape[2:], lambda b, h, i: (b, h, 0, 0))
    return pl.pallas_call(
        functools.partial(_attn_kernel, nseg=nseg, scale=QK_DIM ** -0.5),
        grid=(B, Hm, nq),
        in_specs=[pl.BlockSpec((1, 1, tq, QK_DIM), lambda b, h, i: (b, h, i, 0))]
                 + [kspec(a) for a in ks] + [kspec(a) for a in vs],
        out_specs=pl.BlockSpec((tq, HEAD_DIM), lambda b, h, i: (b * nq + i, h)),
        out_shape=jax.ShapeDtypeStruct((B * Tq, Hm * HEAD_DIM), BF16),
        compiler_params=_params("parallel", "parallel", "parallel"),
        name=f"attention_{nseg}seg",
    )(q, *ks, *vs)


def _merge_kernel(ac_ref, ag_ref, am_ref, wc_ref, wg_ref, wm_ref, gc_ref, gg_ref, gm_ref, o_ref):
    sig = lambda r: jax.nn.sigmoid(r[...].astype(F32))
    m = (sig(gc_ref) * _dot(ac_ref[...], wc_ref[...]) + sig(gg_ref) * _dot(ag_ref[...], wg_ref[...])
         + sig(gm_ref) * _dot(am_ref[...], wm_ref[...]))
    o_ref[...] = m.astype(o_ref.dtype)


def merge_branches(ac, ag, am, wc, wg, wm, u, off):
    N = ac.shape[0]
    D = wc.shape[1]
    tm = _pick(N, (1024, 512, 256, 128))
    tn = _pick((D, off["gate_conv"], off["gate_gdn"], off["gate_mla"]), (512, 256, 128))
    act = lambda a: pl.BlockSpec((tm, a.shape[1]), lambda i, j: (i, 0))
    wsp = lambda w: pl.BlockSpec((w.shape[0], tn), lambda i, j: (0, j))
    gate = lambda name: pl.BlockSpec((tm, tn), lambda i, j: (i, off[name] // tn + j))
    return pl.pallas_call(
        _merge_kernel,
        grid=(N // tm, D // tn),
        in_specs=[act(ac), act(ag), act(am), wsp(wc), wsp(wg), wsp(wm),
                  gate("gate_conv"), gate("gate_gdn"), gate("gate_mla")],
        out_specs=pl.BlockSpec((tm, tn), lambda i, j: (i, j)),
        out_shape=jax.ShapeDtypeStruct((N, D), BF16),
        compiler_params=_params("parallel", "parallel"),
        name="merge_branches",
    )(ac, ag, am, wc, wg, wm, u, u, u)


def _out_proj_kernel(m_ref, w_ref, h_ref, g_ref, o_ref):
    o_ref[...] = h_ref[...] + g_ref[0] * _dot(m_ref[...], w_ref[...])


def out_projection(m, w_o, h, gate, rows_per_group):
    N, D = h.shape
    tm = _pick(rows_per_group, (1024, 512, 256, 128))
    tn = _pick(D, (512, 256, 128))
    gdiv = rows_per_group // tm
    return pl.pallas_call(
        _out_proj_kernel,
        grid=(N // tm, D // tn),
        in_specs=[pl.BlockSpec((tm, m.shape[1]), lambda i, j: (i, 0)),
                  pl.BlockSpec((m.shape[1], tn), lambda i, j: (0, j)),
                  pl.BlockSpec((tm, tn), lambda i, j: (i, j)),
                  pl.BlockSpec((1, 1, tn), lambda i, j: (i // gdiv, 0, j))],
        out_specs=pl.BlockSpec((tm, tn), lambda i, j: (i, j)),
        out_shape=jax.ShapeDtypeStruct((N, D), F32),
        input_output_aliases={2: 0},
        compiler_params=_params("parallel", "parallel"),
        name="out_projection",
    )(m, w_o, h, gate)


def _router_kernel(x_ref, g_ref, sh_ref, sc_ref, r_ref, hn_ref, lg_ref):
    a = _norm_mod(x_ref[...], g_ref[...], sh_ref[0], sc_ref[0]).astype(BF16)
    hn_ref[...] = a
    lg_ref[0] = _dot_nt(r_ref[...], a)


def router(x, g, sh, sc, router_t, B, T, rows_per_group):
    N, D = x.shape
    E = router_t.shape[0]
    tm = _pick(T, (1024, 512, 256, 128))
    nt = T // tm
    gdiv = rows_per_group // tm
    return pl.pallas_call(
        _router_kernel,
        grid=(N // tm,),
        in_specs=[pl.BlockSpec((tm, D), lambda i: (i, 0)),
                  pl.BlockSpec((1, D), lambda i: (0, 0)),
                  pl.BlockSpec((1, 1, D), lambda i: (i // gdiv, 0, 0)),
                  pl.BlockSpec((1, 1, D), lambda i: (i // gdiv, 0, 0)),
                  pl.BlockSpec((E, D), lambda i: (0, 0))],
        out_specs=[pl.BlockSpec((tm, D), lambda i: (i, 0)),
                   pl.BlockSpec((1, E, tm), lambda i: (i // nt, 0, i % nt))],
        out_shape=[jax.ShapeDtypeStruct((N, D), BF16), jax.ShapeDtypeStruct((B, E, T), F32)],
        compiler_params=_params("parallel"),
        name="router",
    )(x, g.reshape(1, D), sh, sc, router_t)


def _route_kernel(lg_ref, rm_ref, aff_ref, rmt_ref, *, T, E, cap, W):
    lg = lg_ref[0]
    mx = jnp.max(lg, axis=0, keepdims=True)
    ex = jnp.exp(lg - mx)
    aff = ex / jnp.sum(ex, axis=0, keepdims=True)
    aff_ref[0] = aff
    bits = lax.bitcast_convert_type(aff, jnp.int32)

    def count(mask):
        return jnp.sum(jnp.where(mask, 1.0, 0.0), axis=1, keepdims=True)

    thr = jnp.zeros((E, 1), jnp.int32)
    for bit in range(30, -1, -1):
        cand = thr | (1 << bit)
        thr = jnp.where(count(bits >= cand) >= cap, cand, thr)
    gt = bits > thr
    eq = bits == thr
    need = cap - count(gt)

    ri = lax.broadcasted_iota(jnp.int32, (W, W), 0)
    ci = lax.broadcasted_iota(jnp.int32, (W, W), 1)
    upper = jnp.where(ri < ci, 1.0, 0.0).astype(BF16)
    ident = jnp.where(ri == ci, 1.0, 0.0).astype(BF16)

    def excl_cumsum(mask):
        m = jnp.where(mask, 1.0, 0.0)
        outs, carry = [], jnp.zeros((E, 1), F32)
        for j in range(T // W):
            blk = m[:, j * W:(j + 1) * W]
            outs.append(_dot(blk.astype(BF16), upper) + carry)
            carry = carry + jnp.sum(blk, axis=1, keepdims=True)
        return jnp.concatenate(outs, axis=1) if len(outs) > 1 else outs[0]

    sel = gt | (eq & (excl_cumsum(eq) < need))
    rm = jnp.where(sel, excl_cumsum(sel), -1.0)
    rm_ref[0] = rm
    rmb = rm.astype(BF16)
    for j in range(T // W):
        rmt_ref[0, j * W:(j + 1) * W, :] = _dot_nt(ident, rmb[:, j * W:(j + 1) * W])


def route(logits_t, cap):
    B, E, T = logits_t.shape
    W = 256 if T % 256 == 0 else 128
    blk = pl.BlockSpec((1, E, T), lambda b: (b, 0, 0))
    return pl.pallas_call(
        functools.partial(_route_kernel, T=T, E=E, cap=cap, W=W),
        grid=(B,),
        in_specs=[blk],
        out_specs=[blk, blk, pl.BlockSpec((1, T, E), lambda b: (b, 0, 0))],
        out_shape=[jax.ShapeDtypeStruct((B, E, T), F32), jax.ShapeDtypeStruct((B, E, T), F32),
                   jax.ShapeDtypeStruct((B, T, E), F32)],
        compiler_params=_params("parallel"),
        name="route",
    )(logits_t)


def _gather_kernel(rm_ref, aff_ref, hn_ref, xs_ref, gate_ref, *, cap):
    rm = rm_ref[0, 0]
    T = rm.shape[1]
    slot = lax.broadcasted_iota(jnp.int32, (cap, T), 0).astype(F32)
    hit = slot == rm
    xs_ref[0, 0] = _dot(jnp.where(hit, 1.0, 0.0).astype(BF16), hn_ref[...]).astype(xs_ref.dtype)
    gate_ref[0, 0] = jnp.sum(jnp.where(hit, aff_ref[0, 0], 0.0), axis=1, keepdims=True)


def moe_gather(rm, aff, hn, cap):
    B, E, T = rm.shape
    D = hn.shape[1]
    rowspec = pl.BlockSpec((1, 1, 1, T), lambda b, e: (b, e, 0, 0))
    return pl.pallas_call(
        functools.partial(_gather_kernel, cap=cap),
        grid=(B, E),
        in_specs=[rowspec, rowspec, pl.BlockSpec((T, D), lambda b, e: (b, 0))],
        out_specs=[pl.BlockSpec((1, 1, cap, D), lambda b, e: (b, e, 0, 0)),
                   pl.BlockSpec((1, 1, cap, 1), lambda b, e: (b, e, 0, 0))],
        out_shape=[jax.ShapeDtypeStruct((B, E, cap, D), BF16), jax.ShapeDtypeStruct((B, E, cap, 1), F32)],
        compiler_params=_params("parallel", "parallel"),
        name="moe_gather",
    )(rm.reshape(B, E, 1, T), aff.reshape(B, E, 1, T), hn)


def _ffn_kernel(xs_ref, gate_ref, wg_ref, wu_ref, wd_ref, ye_ref):
    bb, _, cap, D = xs_ref.shape
    xs = xs_ref[...].reshape(bb * cap, D)
    hid = (_silu(_dot(xs, wg_ref[0])) * _dot(xs, wu_ref[0])).astype(BF16)
    ye = _dot(hid, wd_ref[0]) * gate_ref[...].reshape(bb * cap, 1)
    ye_ref[...] = ye.reshape(bb, 1, cap, D).astype(ye_ref.dtype)


def moe_ffn(xs, gate, w_gate, w_up, w_down):
    B, E, cap, D = xs.shape
    FF = w_gate.shape[2]
    bb = max(1, min(B, 512 // cap))
    while B % bb:
        bb -= 1
    tok = lambda last: pl.BlockSpec((bb, 1, cap, last), lambda e, b: (b, e, 0, 0))
    return pl.pallas_call(
        _ffn_kernel,
        grid=(E, B // bb),
        in_specs=[tok(D), tok(1),
                  pl.BlockSpec((1, D, FF), lambda e, b: (e, 0, 0)),
                  pl.BlockSpec((1, D, FF), lambda e, b: (e, 0, 0)),
                  pl.BlockSpec((1, FF, D), lambda e, b: (e, 0, 0))],
        out_specs=tok(D),
        out_shape=jax.ShapeDtypeStruct((B, E, cap, D), BF16),
        compiler_params=_params("parallel", "parallel"),
        name="moe_ffn",
    )(xs, gate, w_gate, w_up, w_down)


def _scatter_kernel(rmt_ref, ye_ref, h_ref, g_ref, o_ref, *, E, cap):
    tt = rmt_ref.shape[1]
    rmt = rmt_ref[0]
    slot = lax.broadcasted_iota(jnp.int32, (tt, cap), 1).astype(F32)
    acc = jnp.zeros(o_ref.shape, F32)
    for e in range(E):
        onehot = jnp.where(rmt[:, e:e + 1] == slot, 1.0, 0.0).astype(BF16)
        acc = acc + _dot(onehot, ye_ref[0, e * cap:(e + 1) * cap, :])
    o_ref[...] = h_ref[...] + g_ref[0] * acc


def moe_scatter(rmt, ye, h, gate, T):
    B, _, E = rmt.shape
    _, EC, D = ye.shape
    cap = EC // E
    tt = _pick(T, (512, 256, 128))
    tn = _pick(D, (1024, 512, 256, 128))
    nt = T // tt
    return pl.pallas_call(
        functools.partial(_scatter_kernel, E=E, cap=cap),
        grid=(B, D // tn, nt),
        in_specs=[pl.BlockSpec((1, tt, E), lambda b, j, i: (b, i, 0)),
                  pl.BlockSpec((1, EC, tn), lambda b, j, i: (b, 0, j)),
                  pl.BlockSpec((tt, tn), lambda b, j, i: (b * nt + i, j)),
                  pl.BlockSpec((1, 1, tn), lambda b, j, i: (b, 0, j))],
        out_specs=pl.BlockSpec((tt, tn), lambda b, j, i: (b * nt + i, j)),
        out_shape=jax.ShapeDtypeStruct(h.shape, F32),
        input_output_aliases={2: 0},
        compiler_params=_params("parallel", "parallel", "parallel"),
        name="moe_scatter",
    )(rmt, ye, h, gate)


def expert_choice_moe(h, g, sh, sc, gate, router_t, w_gate, w_up, w_down, B, T, rows_per_group):
    E = router_t.shape[0]
    cap = EC_CAPACITY * T // E
    assert cap <= 256 and cap % 16 == 0, "slot indices must stay exact in bf16 and fill bf16 sublane tiles"
    hn, logits_t = router(h, g, sh, sc, router_t, B, T, rows_per_group)
    rm, aff, rmt = route(logits_t, cap)
    xs, gsl = moe_gather(rm, aff, hn, cap)
    ye = moe_ffn(xs, gsl, w_gate, w_up, w_down)
    return moe_scatter(rmt, ye.reshape(B, E * cap, ye.shape[-1]), h, gate, T)


def _final_norm_kernel(x_ref, g_ref, o_ref):
    o_ref[...] = _rms(x_ref[...], g_ref[...])


def final_norm(x, g):
    N, D = x.shape
    tm = _pick(N, (512, 256, 128))
    return pl.pallas_call(
        _final_norm_kernel,
        grid=(N // tm,),
        in_specs=[pl.BlockSpec((tm, D), lambda i: (i, 0)), pl.BlockSpec((1, D), lambda i: (0, 0))],
        out_specs=pl.BlockSpec((tm, D), lambda i: (i, 0)),
        out_shape=jax.ShapeDtypeStruct((N, D), F32),
        compiler_params=_params("parallel"),
        name="final_norm",
    )(x, g.reshape(1, D))


def _swap_rope(w):
    half = ROPE_DIM // 2
    return jnp.concatenate([-w[..., half:], w[..., :half]], axis=-1)


_PACKED_ORDER = ("mla_cq", "gdn_small", "mla_kpe", "mla_ckv", "conv_x", "conv_b", "conv_c", "gdn_q", "gdn_k", "gdn_v",
                 "gdn_z", "gate_conv", "gate_gdn", "gate_mla")


def _in_layout(C, Hg, RQ, RKV, D):
    width = {"mla_cq": RQ, "gdn_small": LANE, "mla_kpe": LANE, "mla_ckv": RKV, "conv_x": C, "conv_b": C, "conv_c": C,
             "gdn_q": Hg * HEAD_DIM, "gdn_k": Hg * HEAD_DIM, "gdn_v": Hg * HEAD_DIM, "gdn_z": Hg * HEAD_DIM,
             "gate_conv": D, "gate_gdn": D, "gate_mla": D}
    off, o = {}, 0
    for name in _PACKED_ORDER:
        off[name] = o
        o += width[name]
    assert off["mla_cq"] % RQ == 0 and off["mla_ckv"] % RKV == 0
    return off, o


def _pack_w_in(w, C, Hg, RQ, RKV, D):
    names = ("conv_x", "conv_b", "conv_c", "gdn_q", "gdn_k", "gdn_v", "gdn_z", "gdn_small", "mla_cq", "mla_ckv",
             "mla_kpe", "gate_conv", "gate_gdn", "gate_mla")
    sizes = (C, C, C, Hg * HEAD_DIM, Hg * HEAD_DIM, Hg * HEAD_DIM, Hg * HEAD_DIM, 4 * Hg, RQ, RKV, ROPE_DIM, D, D, D)
    seg, o = {}, 0
    for name, size in zip(names, sizes):
        seg[name] = w[:, o:o + size]
        o += size
    seg["gdn_small"] = jnp.pad(seg["gdn_small"], ((0, 0), (0, LANE - 4 * Hg)))
    seg["mla_kpe"] = jnp.concatenate([seg["mla_kpe"], _swap_rope(seg["mla_kpe"])], axis=1)
    return jnp.concatenate([seg[n] for n in _PACKED_ORDER], axis=1).astype(BF16)


def _rope_tables(T):
    rows = T // GRID_W
    row = jnp.repeat(jnp.arange(rows), GRID_W)
    col = jnp.tile(jnp.arange(GRID_W), rows)
    n_freq = ROPE_DIM // 4
    inv_freq = 1.0 / (ROPE_THETA ** (jnp.arange(n_freq, dtype=F32) / n_freq))
    ang = jnp.concatenate([row[:, None] * inv_freq, col[:, None] * inv_freq], axis=-1)
    cos, sin = jnp.cos(ang), jnp.sin(ang)
    return jnp.concatenate([cos, cos], axis=1), jnp.concatenate([sin, sin], axis=1)


def kernel(x, c, ctx, c_ctx, w_ada, b_ada, norm1_g, w_in, conv_w, conv_out, gdn_conv_w, gdn_a_log, gdn_dt_bias,
           gdn_norm_g, gdn_out, mla_q_norm_g, mla_w_uq, mla_kv_norm_g, mla_w_ukv, mla_out, w_o, norm2_g, router_w,
           w_gate, w_up, w_down, final_g):
    B, T, D = x.shape
    TC = ctx.shape[1]
    L = w_ada.shape[0]
    C = conv_w.shape[-1]
    Hg = gdn_a_log.shape[-1]
    RQ = mla_q_norm_g.shape[-1]
    RKV = mla_kv_norm_g.shape[-1]
    Hm = mla_w_uq.shape[-1] // QK_DIM
    assert 4 * Hg <= LANE and T % GDN_CHUNK == 0 and TC % GDN_CHUNK == 0

    off, _ = _in_layout(C, Hg, RQ, RKV, D)

    R = -(-(B + 1) // 8) * 8
    cc = jnp.concatenate([c, c_ctx[None, :], jnp.zeros((R - B - 1, D), F32)], axis=0)
    mod = ada_modulation(cc, w_ada, b_ada)

    cos64, sin64 = _rope_tables(T)
    ones_h, zeros_h = jnp.ones((T, HEAD_DIM), F32), jnp.zeros((T, HEAD_DIM), F32)
    q_ct, q_st = jnp.concatenate([ones_h, cos64], 1), jnp.concatenate([zeros_h, sin64], 1)
    q_ct0, q_st0 = jnp.ones((TC, QK_DIM), F32), jnp.zeros((TC, QK_DIM), F32)
    k_tab = jnp.concatenate([cos64, sin64], 1)
    k_tab0 = jnp.concatenate([jnp.ones((TC, ROPE_DIM), F32), jnp.zeros((TC, ROPE_DIM), F32)], 1)

    h = x.reshape(B * T, D)
    hc = ctx.reshape(B * TC, D)
    zero_state = jnp.zeros((B, Hg, HEAD_DIM, HEAD_DIM), F32)

    for l in range(L):
        need_ctx = l < L - 1
        w_in_l = _pack_w_in(w_in[l], C, Hg, RQ, RKV, D)
        wq = mla_w_uq[l].reshape(RQ, Hm, QK_DIM).transpose(1, 0, 2)
        wq_main = wq.astype(BF16)
        wq_aux = jnp.concatenate([jnp.zeros((Hm, RQ, HEAD_DIM), F32), _swap_rope(wq[..., HEAD_DIM:])], -1).astype(BF16)
        wkv = mla_w_ukv[l].reshape(RKV, Hm, 2 * HEAD_DIM).transpose(1, 0, 2)
        wk, wv = wkv[..., :HEAD_DIM].astype(BF16), wkv[..., HEAD_DIM:].astype(BF16)
        pad = jnp.zeros((LANE - 2 * Hg,), F32)
        alog_row = jnp.concatenate([gdn_a_log[l].reshape(-1), pad]).reshape(1, LANE)
        dtb_row = jnp.concatenate([gdn_dt_bias[l].reshape(-1), pad]).reshape(1, LANE)
        wc_o, wg_o, wm_o, wo_l = (a[l].astype(BF16) for a in (conv_out, gdn_out, mla_out, w_o))
        router_t = router_w[l].T.astype(BF16)
        wgt, wup, wdn = w_gate[l].astype(BF16), w_up[l].astype(BF16), w_down[l].astype(BF16)

        ml = mod[l, :B].reshape(B, 1, 6 * D)
        mc = mod[l, B:B + 1].reshape(1, 1, 6 * D)
        sh1, sc1, g1, sh2, sc2, g2 = (ml[..., i * D:(i + 1) * D] for i in range(6))
        csh1, csc1, cg1, csh2, csc2, cg2 = (mc[..., i * D:(i + 1) * D] for i in range(6))

        u_lat = in_projection(h, norm1_g[l], sh1, sc1, w_in_l, T)
        u_ctx = in_projection(hc, norm1_g[l], csh1, csc1, w_in_l, B * TC)

        ag_ctx, s_f, s_b = gdn_mixer(u_ctx, gdn_conv_w[l], alog_row, dtb_row, gdn_norm_g[l], zero_state, zero_state,
                                     B, TC, Hg, off)
        ag_lat, _, _ = gdn_mixer(u_lat, gdn_conv_w[l], alog_row, dtb_row, gdn_norm_g[l], s_f, s_b, B, T, Hg, off)

        k_ctx, v_ctx = mla_keys_values(u_ctx, mla_kv_norm_g[l], wk, wv, k_tab0, B, TC, off)
        k_lat, v_lat = mla_keys_values(u_lat, mla_kv_norm_g[l], wk, wv, k_tab, B, T, off)
        q_lat = mla_queries(u_lat, mla_q_norm_g[l], wq_main, wq_aux, q_ct, q_st, B, T, off)
        am_lat = attention(q_lat, [k_ctx, k_lat], [v_ctx, v_lat])

        ac_lat = conv_mixer_front(u_lat, conv_w[l], B, T, off)
        m_lat = merge_branches(ac_lat, ag_lat, am_lat, wc_o, wg_o, wm_o, u_lat, off)
        h = out_projection(m_lat, wo_l, h, g1, T)
        h = expert_choice_moe(h, norm2_g[l], sh2, sc2, g2, router_t, wgt, wup, wdn, B, T, T)

        if need_ctx:
            q_ctx = mla_queries(u_ctx, mla_q_norm_g[l], wq_main, wq_aux, q_ct0, q_st0, B, TC, off)
            am_ctx = attention(q_ctx, [k_ctx], [v_ctx])
            ac_ctx = conv_mixer_front(u_ctx, conv_w[l], B, TC, off)
            m_ctx = merge_branches(ac_ctx, ag_ctx, am_ctx, wc_o, wg_o, wm_o, u_ctx, off)
            hc = out_projection(m_ctx, wo_l, hc, cg1, B * TC)
            cg2b = jnp.broadcast_to(cg2, (B, 1, D))
            hc = expert_choice_moe(hc, norm2_g[l], csh2, csc2, cg2b, router_t, wgt, wup, wdn, B, TC, B * TC)

    return final_norm(h, final_g).reshape(B, T, D)
```

```python
import functools
import math

import jax
import jax.numpy as jnp
from jax import lax
from jax.experimental import pallas as pl
from jax.experimental.pallas import tpu as pltpu

F32 = jnp.float32
BF16 = jnp.bfloat16

RMS_EPS = 1e-6
L2_EPS = 1e-6
ROPE_THETA = 10000.0
GRID_W = 64
HEAD_DIM = 128
ROPE_DIM = 64
QK_DIM = HEAD_DIM + ROPE_DIM
GDN_CHUNK = 64
GDN_SUB = 16
EC_CAPACITY = 2
LANE = 128
VMEM_LIMIT = 56 * 1024 * 1024


def _pick(n, cands):
    ns = n if isinstance(n, tuple) else (n,)
    for c in cands:
        if all(v % c == 0 for v in ns):
            return c
    raise ValueError(f"no tile in {cands} divides {ns}")


def _params(*sem):
    return pltpu.CompilerParams(dimension_semantics=sem, vmem_limit_bytes=VMEM_LIMIT)


def _dot(a, b):
    return jnp.dot(a, b, preferred_element_type=F32)


def _dot_nt(a, b):
    return lax.dot_general(a, b, (((1,), (1,)), ((), ())), preferred_element_type=F32)


def _dot_tn(a, b):
    return lax.dot_general(a, b, (((0,), (0,)), ((), ())), preferred_element_type=F32)


def _split2(x):
    hi = x.astype(BF16)
    lo = (x - hi.astype(F32)).astype(BF16)
    return hi, lo


def _split3(x):
    hi = x.astype(BF16)
    r = x - hi.astype(F32)
    mid = r.astype(BF16)
    lo = (r - mid.astype(F32)).astype(BF16)
    return hi, mid, lo


def _mm3(a, b):
    ah, al = _split2(a)
    bh, bl = _split2(b)
    return _dot(ah, bh) + (_dot(ah, bl) + _dot(al, bh))


def _silu(x):
    return x * jax.nn.sigmoid(x)


def _softplus(x):
    return jnp.maximum(x, 0.0) + jnp.log(1.0 + jnp.exp(-jnp.abs(x)))


def _ada_kernel(c_ref, w_ref, b_ref, o_ref):
    a = _silu(c_ref[...])
    ah, al = _split2(a)
    wh, wl = _split2(w_ref[0])
    o_ref[0] = _dot(ah, wh) + (_dot(ah, wl) + _dot(al, wh)) + b_ref[0]


def ada_modulation(cc, w_ada, b_ada):
    L, D, D6 = w_ada.shape
    R = cc.shape[0]
    tn = _pick(D6, (1024, 512, 256, 128))
    return pl.pallas_call(
        _ada_kernel,
        grid=(L, D6 // tn),
        in_specs=[pl.BlockSpec((R, D), lambda l, j: (0, 0)),
                  pl.BlockSpec((1, D, tn), lambda l, j: (l, 0, j)),
                  pl.BlockSpec((1, 1, tn), lambda l, j: (l, 0, j))],
        out_specs=pl.BlockSpec((1, R, tn), lambda l, j: (l, 0, j)),
        out_shape=jax.ShapeDtypeStruct((L, R, D6), F32),
        compiler_params=_params("parallel", "parallel"),
        name="ada_modulation",
    )(cc, w_ada, b_ada.reshape(L, 1, D6))


def _norm_mod(x, g, sh, sc):
    ms = jnp.mean(x * x, axis=-1, keepdims=True)
    y = x * lax.rsqrt(ms + RMS_EPS) * g
    return y * (1.0 + sc) + sh


def _in_proj_kernel(x_ref, g_ref, sh_ref, sc_ref, w_ref, o_ref, a_scr):
    @pl.when(pl.program_id(1) == 0)
    def _():
        a_scr[...] = _norm_mod(x_ref[...], g_ref[...], sh_ref[0], sc_ref[0]).astype(BF16)

    o_ref[...] = _dot(a_scr[...], w_ref[...]).astype(o_ref.dtype)


def in_projection(x, g, sh, sc, w, rows_per_group):
    N, D = x.shape
    NC = w.shape[1]
    tm = _pick(rows_per_group, (1024, 512, 256, 128))
    tn = _pick(NC, (512, 256, 128))
    gdiv = rows_per_group // tm
    return pl.pallas_call(
        _in_proj_kernel,
        grid=(N // tm, NC // tn),
        in_specs=[pl.BlockSpec((tm, D), lambda i, j: (i, 0)),
                  pl.BlockSpec((1, D), lambda i, j: (0, 0)),
                  pl.BlockSpec((1, 1, D), lambda i, j: (i // gdiv, 0, 0)),
                  pl.BlockSpec((1, 1, D), lambda i, j: (i // gdiv, 0, 0)),
                  pl.BlockSpec((D, tn), lambda i, j: (0, j))],
        out_specs=pl.BlockSpec((tm, tn), lambda i, j: (i, j)),
        out_shape=jax.ShapeDtypeStruct((N, NC), BF16),
        scratch_shapes=[pltpu.VMEM((tm, D), BF16)],
        compiler_params=_params("parallel", "arbitrary"),
        name="in_projection",
    )(x, g.reshape(1, D), sh, sc, w)


def _conv3(x, w, T):
    row = lax.broadcasted_iota(jnp.int32, x.shape, 0)
    prev = jnp.where(row == 0, 0.0, pltpu.roll(x, 1, 0))
    nxt = jnp.where(row == T - 1, 0.0, pltpu.roll(x, T - 1, 0))
    return prev * w[0:1] + x * w[1:2] + nxt * w[2:3]


def _conv_mixer_kernel(x_ref, b_ref, c_ref, w_ref, o_ref, *, T):
    v = c_ref[...].astype(F32) * x_ref[...].astype(F32)
    o_ref[...] = (b_ref[...].astype(F32) * _conv3(v, w_ref[...], T)).astype(o_ref.dtype)


def conv_mixer_front(u, conv_w, B, T, off):
    C = conv_w.shape[1]
    tc = _pick(C, (256, 128))
    ox, ob, oc = (off[k] // tc for k in ("conv_x", "conv_b", "conv_c"))
    return pl.pallas_call(
        functools.partial(_conv_mixer_kernel, T=T),
        grid=(B, C // tc),
        in_specs=[pl.BlockSpec((T, tc), lambda b, j: (b, ox + j)),
                  pl.BlockSpec((T, tc), lambda b, j: (b, ob + j)),
                  pl.BlockSpec((T, tc), lambda b, j: (b, oc + j)),
                  pl.BlockSpec((3, tc), lambda b, j: (0, j))],
        out_specs=pl.BlockSpec((T, tc), lambda b, j: (b, j)),
        out_shape=jax.ShapeDtypeStruct((B * T, C), BF16),
        compiler_params=_params("parallel", "parallel"),
        name="conv_mixer_front",
    )(u, u, u, conv_w)


def _gdn_solve(a_list, rhs_list, sub_mask, eye):
    g = a_list[0].shape[0]
    b16 = lambda ts: [t.astype(BF16) for t in ts]
    dots = lambda xs, ys: [_dot(x, y) for x, y in zip(xs, ys)]
    d = [jnp.where(sub_mask, a, 0.0) for a in a_list]
    n = [a - dd for a, dd in zip(a_list, d)]
    db = b16(d)
    d2b = b16(dots(db, db))
    d4b = b16(dots(d2b, d2b))
    d8b = b16(dots(d4b, d4b))
    x = [eye - dd for dd in d]
    for p in (d2b, d4b, d8b):
        x = [xx + t for xx, t in zip(x, dots(b16(x), p))]
    z = dots(b16(x), b16([jnp.concatenate([nn, r], axis=1) for nn, r in zip(n, rhs_list)]))
    mb = b16([zz[:, :g] for zz in z])
    zr = [zz[:, g:] for zz in z]
    m2b = b16(dots(mb, mb))
    zr = [r + t for r, t in zip(zr, dots(m2b, b16(zr)))]
    return [r - t for r, t in zip(zr, dots(mb, b16(zr)))]


def _gdn_kernel(q_ref, k_ref, v_ref, z_ref, ab_ref, wq_ref, wk_ref, wv_ref, alog_ref, dtb_ref, ng_ref,
                s0f_ref, s0b_ref, o_ref, sf_ref, sb_ref,
                q_s, k_s, v_s, col_s, u_s, w_s, qg_s, kd_s, in_s, eg_s, o_s, st_s, *, T, H, G, GI, HB):
    hb = pl.program_id(1)
    NC = T // GDN_CHUNK
    L = GDN_CHUNK

    def l2n(x):
        return x * lax.rsqrt(jnp.sum(x * x, axis=-1, keepdims=True) + L2_EPS)

    ri = lax.broadcasted_iota(jnp.int32, (G, G), 0)
    ci = lax.broadcasted_iota(jnp.int32, (G, G), 1)
    same_chunk = (ri >> int(math.log2(L))) == (ci >> int(math.log2(L)))
    same_sub = (ri >> int(math.log2(GDN_SUB))) == (ci >> int(math.log2(GDN_SUB)))
    eye = jnp.where(ri == ci, 1.0, 0.0).astype(F32)
    ones_chunk = jnp.where(same_chunk, 1.0, 0.0).astype(BF16)
    ones_8g = jnp.ones((8, G), BF16)

    def exact3(m_bf, x):
        xh, xm, xl = _split3(x)
        return _dot(m_bf, xh) + _dot(m_bf, xm) + _dot(m_bf, xl)

    for hh in range(HB):
        hsl = slice(hh * HEAD_DIM, (hh + 1) * HEAD_DIM)
        head = hb * HB + hh

        def prep(x_ref, w_ref):
            return _silu(_conv3(x_ref[:, hsl].astype(F32), w_ref[:, hsl], T))

        q_s[hh] = l2n(prep(q_ref, wq_ref)) * (HEAD_DIM ** -0.5)
        k_s[hh] = l2n(prep(k_ref, wk_ref))
        v_s[hh] = prep(v_ref, wv_ref)

        ab = ab_ref[...].astype(F32)
        lane = lax.broadcasted_iota(jnp.int32, ab.shape, 1)
        gfull = -jnp.exp(alog_ref[...]) * _softplus(ab + dtb_ref[...])
        bfull = jax.nn.sigmoid(ab)

        def col(x, idx):
            return jnp.sum(jnp.where(lane == idx, x, 0.0), axis=1, keepdims=True)

        g_f, g_b = col(gfull, head), col(gfull, H + head)
        b_f, b_b = col(bfull, 2 * H + head), col(bfull, 3 * H + head)
        col_s[hh] = jnp.where(lane == 0, g_f, jnp.where(lane == 1, g_b, jnp.where(lane == 2, b_f, b_b)))
        st_s[hh, 0] = s0f_ref[0, hh]
        st_s[hh, 1] = s0b_ref[0, hh]

    chains = [(hh, d) for hh in range(HB) for d in range(2)]
    incl = [same_chunk & (ci <= ri), same_chunk & (ci >= ri)]
    strict = [same_chunk & (ci < ri), same_chunk & (ci > ri)]
    incl_b = [jnp.where(m, 1.0, 0.0).astype(BF16) for m in incl]

    def exact3_list(ms, xs):
        parts = [_split3(x) for x in xs]
        return [_dot(m, p[0]) + _dot(m, p[1]) + _dot(m, p[2]) for m, p in zip(ms, parts)]

    def group_body(gi, carry):
        units = [(j, hh) for j in range(GI) for hh in range(HB)]
        gch = [(ui, d) for ui in range(len(units)) for d in range(2)]
        r0 = [pl.multiple_of((gi * GI + j) * G, G) for j in range(GI)]
        rows = [pl.ds(r0[j], G) for j, _ in units]
        qg = [q_s[hh, r, :] for (_, hh), r in zip(units, rows)]
        kg = [k_s[hh, r, :] for (_, hh), r in zip(units, rows)]
        vg = [v_s[hh, r, :] for (_, hh), r in zip(units, rows)]
        cols = [col_s[hh, r, :] for (_, hh), r in zip(units, rows)]
        kb16 = [k.astype(BF16) for k in kg]
        kk = [_dot_nt(kb, kb) for kb in kb16]
        qk = [_dot_nt(q.astype(BF16), kb) for q, kb in zip(qg, kb16)]
        bcol = [cols[ui][:, 2 + d:3 + d] for ui, d in gch]
        gb = [jnp.broadcast_to(cols[ui][:, d:d + 1], (G, LANE)) for ui, d in gch]
        gc = exact3_list([incl_b[d] for _, d in gch], gb)
        gl = exact3_list([ones_chunk] * len(gch), gb)
        cmat = [jnp.concatenate([g] * (G // LANE), axis=1) if G > LANE else g for g in gc]
        rrow = [r[0:1] for r in exact3_list([ones_8g] * len(gch),
                                            [jnp.where(ri == ci, cm, 0.0) for cm in cmat])]
        decay = [jnp.where(incl[d], jnp.exp(jnp.where(incl[d], cm - rr, 0.0)), 0.0)
                 for (_, d), cm, rr in zip(gch, cmat, rrow)]
        a = [jnp.where(strict[d], bc * kk[ui] * dc, 0.0) for (ui, d), bc, dc in zip(gch, bcol, decay)]
        egc = [jnp.exp(g) for g in gc]
        rhs = [jnp.concatenate([vg[ui] * bc, kg[ui] * bc * e], axis=1) for (ui, _), bc, e in zip(gch, bcol, egc)]
        uw = _gdn_solve(a, rhs, same_sub, eye)
        for i, (ui, d) in enumerate(gch):
            j, hh = units[ui]
            u_s[hh, d, rows[ui], :] = uw[i][:, :HEAD_DIM]
            w_s[hh, d, rows[ui], :] = uw[i][:, HEAD_DIM:].astype(BF16)
            qg_s[hh, d, rows[ui], :] = (qg[ui] * egc[i]).astype(BF16)
            kd_s[hh, d, rows[ui], :] = (kg[ui] * jnp.exp(gl[i] - gc[i])).astype(BF16)
            egl = jnp.exp(gl[i])
            intra = (qk[ui] * decay[i]).astype(BF16)
            for c in range(G // L):
                e0 = pl.multiple_of((gi * GI + j) * (G // 8) + c * 8, 8)
                eg_s[hh, d, pl.ds(e0, 8), :] = egl[c * L:c * L + 8]
                in_s[hh, d, pl.ds(r0[j] + c * L, L), :] = intra[c * L:(c + 1) * L, c * L:(c + 1) * L]
        return carry

    lax.fori_loop(0, T // (G * GI), group_body, 0)

    def scan_body(c, carry):
        rws = [pl.ds(pl.multiple_of((c if d == 0 else NC - 1 - c) * L, L), L) for _, d in chains]
        egr = [pl.ds(pl.multiple_of((c if d == 0 else NC - 1 - c) * 8, 8), 8) for _, d in chains]
        s = [st_s[hh, d] for hh, d in chains]
        wq = [jnp.concatenate([w_s[hh, d, r, :], qg_s[hh, d, r, :]], axis=0) for (hh, d), r in zip(chains, rws)]
        ws = [_dot(x, y.astype(BF16)) for x, y in zip(wq, s)]
        vb = [(u_s[hh, d, r, :] - w[:L]).astype(BF16) for (hh, d), r, w in zip(chains, rws, ws)]
        oi = [_dot(in_s[hh, d, r, :], v) for (hh, d), r, v in zip(chains, rws, vb)]
        ds_ = [_dot_tn(kd_s[hh, d, r, :], v) for (hh, d), r, v in zip(chains, rws, vb)]
        for i, (hh, d) in enumerate(chains):
            o_s[hh, d, rws[i], :] = ws[i][L:] + oi[i]
            st_s[hh, d] = s[i] * eg_s[hh, d, egr[i], :][0:1] + ds_[i]
        return carry

    lax.fori_loop(0, NC, scan_body, 0)

    for hh in range(HB):
        hsl = slice(hh * HEAD_DIM, (hh + 1) * HEAD_DIM)
        sf_ref[0, hh] = st_s[hh, 0]
        sb_ref[0, hh] = st_s[hh, 1]
        o = o_s[hh, 0] + o_s[hh, 1]
        y = o * lax.rsqrt(jnp.mean(o * o, axis=-1, keepdims=True) + RMS_EPS) * ng_ref[...]
        o_ref[:, hsl] = (y * _silu(z_ref[:, hsl].astype(F32))).astype(o_ref.dtype)


def gdn_mixer(u, conv_w, alog_row, dtb_row, norm_g, s0f, s0b, B, T, H, off):
    G = 2 * GDN_CHUNK
    GI = 2 if T % (2 * G) == 0 else 1
    HB = 2 if H % 2 == 0 else 1
    W = HB * HEAD_DIM
    assert all(off[k] % W == 0 for k in ("gdn_q", "gdn_k", "gdn_v", "gdn_z"))
    oq, ok, ov, oz = (off[k] // W for k in ("gdn_q", "gdn_k", "gdn_v", "gdn_z"))
    oab = off["gdn_small"] // LANE
    seq = lambda o: pl.BlockSpec((T, W), lambda b, h: (b, o + h))
    cw = lambda o: pl.BlockSpec((3, W), lambda b, h: (0, o * (H // HB) + h))
    row = pl.BlockSpec((1, LANE), lambda b, h: (0, 0))
    st = pl.BlockSpec((1, HB, HEAD_DIM, HEAD_DIM), lambda b, h: (b, h, 0, 0))
    tbuf = lambda: pltpu.VMEM((HB, T, HEAD_DIM), F32)
    hbuf = lambda dt: pltpu.VMEM((HB, 2, T, HEAD_DIM), dt)
    return pl.pallas_call(
        functools.partial(_gdn_kernel, T=T, H=H, G=G, GI=GI, HB=HB),
        grid=(B, H // HB),
        in_specs=[seq(oq), seq(ok), seq(ov), seq(oz),
                  pl.BlockSpec((T, LANE), lambda b, h: (b, oab)),
                  cw(0), cw(1), cw(2), row, row, row, st, st],
        out_specs=[pl.BlockSpec((T, W), lambda b, h: (b, h)), st, st],
        out_shape=[jax.ShapeDtypeStruct((B * T, H * HEAD_DIM), BF16),
                   jax.ShapeDtypeStruct((B, H, HEAD_DIM, HEAD_DIM), F32),
                   jax.ShapeDtypeStruct((B, H, HEAD_DIM, HEAD_DIM), F32)],
        scratch_shapes=[tbuf(), tbuf(), tbuf(), tbuf(), hbuf(F32), hbuf(BF16), hbuf(BF16), hbuf(BF16),
                        pltpu.VMEM((HB, 2, T, GDN_CHUNK), BF16), pltpu.VMEM((HB, 2, T // 8, HEAD_DIM), F32),
                        hbuf(F32), pltpu.VMEM((HB, 2, HEAD_DIM, HEAD_DIM), F32)],
        compiler_params=_params("parallel", "parallel"),
        name="gdn_mixer",
    )(u, u, u, u, u, conv_w, conv_w, conv_w, alog_row, dtb_row, norm_g.reshape(1, HEAD_DIM), s0f, s0b)


def _rms(x, g):
    return x * lax.rsqrt(jnp.mean(x * x, axis=-1, keepdims=True) + RMS_EPS) * g


def _mla_q_kernel(x_ref, g_ref, wm_ref, wa_ref, ct_ref, st_ref, o_ref, *, Hm):
    a = _rms(x_ref[...].astype(F32), g_ref[...]).astype(BF16)
    main = _dot(a, wm_ref[...])
    nn = Hm * HEAD_DIM
    pe = main[:, nn:] * ct_ref[...] + _dot(a, wa_ref[...]) * st_ref[...]
    for h in range(Hm):
        o_ref[0, h, :, 0:HEAD_DIM] = main[:, h * HEAD_DIM:(h + 1) * HEAD_DIM].astype(o_ref.dtype)
        o_ref[0, h, :, HEAD_DIM:QK_DIM] = pe[:, h * ROPE_DIM:(h + 1) * ROPE_DIM].astype(o_ref.dtype)


def mla_queries(u, norm_g, w_main, w_aux, ctab, stab, B, T, Hm, off):
    R = w_main.shape[0]
    tm = _pick(T, (512, 256, 128))
    nt = T // tm
    oc = off["mla_cq"] // R
    full = lambda a: pl.BlockSpec(a.shape, lambda i: (0, 0))
    tab = pl.BlockSpec((tm, Hm * ROPE_DIM), lambda i: (i % nt, 0))
    return pl.pallas_call(
        functools.partial(_mla_q_kernel, Hm=Hm),
        grid=(B * nt,),
        in_specs=[pl.BlockSpec((tm, R), lambda i: (i, oc)), pl.BlockSpec((1, R), lambda i: (0, 0)),
                  full(w_main), full(w_aux), tab, tab],
        out_specs=pl.BlockSpec((1, Hm, tm, QK_DIM), lambda i: (i // nt, 0, i % nt, 0)),
        out_shape=jax.ShapeDtypeStruct((B, Hm, T, QK_DIM), BF16),
        compiler_params=_params("parallel"),
        name="mla_queries",
    )(u, norm_g.reshape(1, R), w_main, w_aux, ctab, stab)


def _mla_kv_kernel(x_ref, pe_ref, g_ref, w_ref, tab_ref, k_ref, v_ref, *, Hm):
    a = _rms(x_ref[...].astype(F32), g_ref[...]).astype(BF16)
    kv = _dot(a, w_ref[...])
    r = pe_ref[...].astype(F32) * tab_ref[...]
    pe = (r + pltpu.roll(r, ROPE_DIM, 1))[:, 0:ROPE_DIM].astype(k_ref.dtype)
    for h in range(Hm):
        k_ref[0, h, :, 0:HEAD_DIM] = kv[:, h * HEAD_DIM:(h + 1) * HEAD_DIM].astype(k_ref.dtype)
        k_ref[0, h, :, HEAD_DIM:QK_DIM] = pe
        v_ref[0, h] = kv[:, (Hm + h) * HEAD_DIM:(Hm + h + 1) * HEAD_DIM].astype(v_ref.dtype)


def mla_keys_values(u, norm_g, w_kv, tab, B, T, Hm, off):
    R = w_kv.shape[0]
    tm = _pick(T, (512, 256, 128))
    nt = T // tm
    oc, op = off["mla_ckv"] // R, off["mla_kpe"] // LANE
    return pl.pallas_call(
        functools.partial(_mla_kv_kernel, Hm=Hm),
        grid=(B * nt,),
        in_specs=[pl.BlockSpec((tm, R), lambda i: (i, oc)),
                  pl.BlockSpec((tm, LANE), lambda i: (i, op)),
                  pl.BlockSpec((1, R), lambda i: (0, 0)),
                  pl.BlockSpec(w_kv.shape, lambda i: (0, 0)),
                  pl.BlockSpec((tm, LANE), lambda i: (i % nt, 0))],
        out_specs=[pl.BlockSpec((1, Hm, tm, QK_DIM), lambda i: (i // nt, 0, i % nt, 0)),
                   pl.BlockSpec((1, Hm, tm, HEAD_DIM), lambda i: (i // nt, 0, i % nt, 0))],
        out_shape=[jax.ShapeDtypeStruct((B, Hm, T, QK_DIM), BF16),
                   jax.ShapeDtypeStruct((B, Hm, T, HEAD_DIM), BF16)],
        compiler_params=_params("parallel"),
        name="mla_keys_values",
    )(u, u, norm_g.reshape(1, R), w_kv, tab)


def _attn_kernel(*refs, nseg, scale, nsub):
    q_ref = refs[0]
    k_refs = refs[1:1 + nseg]
    v_refs = refs[1 + nseg:1 + 2 * nseg]
    o_ref = refs[1 + 2 * nseg]
    rs = q_ref.shape[2] // nsub
    c = scale * math.log2(math.e)

    def scores(i):
        q = q_ref[0, 0, i * rs:(i + 1) * rs, :]
        return [_dot_nt(q, k[0, 0]) * c for k in k_refs]

    def finish(i, s):
        m = functools.reduce(jnp.maximum, [jnp.max(x, axis=-1, keepdims=True) for x in s])
        p = [jnp.exp2(x - m) for x in s]
        l = functools.reduce(jnp.add, [jnp.sum(x, axis=-1, keepdims=True) for x in p])
        acc = functools.reduce(jnp.add, [_dot(x.astype(BF16), v[0, 0]) for x, v in zip(p, v_refs)])
        o_ref[i * rs:(i + 1) * rs, :] = (acc / l).astype(o_ref.dtype)

    s_next = scores(0)
    for i in range(nsub):
        s_cur = s_next
        if i + 1 < nsub:
            s_next = scores(i + 1)
        finish(i, s_cur)


def attention(q, ks, vs):
    B, Hm, Tq, _ = q.shape
    tq = _pick(Tq, (512, 256, 128))
    nq = Tq // tq
    nseg = len(ks)
    kspec = lambda a: pl.BlockSpec((1, 1) + a.shape[2:], lambda b, h, i: (b, h, 0, 0))
    return pl.pallas_call(
        functools.partial(_attn_kernel, nseg=nseg, scale=QK_DIM ** -0.5, nsub=tq // 128),
        grid=(B, Hm, nq),
        in_specs=[pl.BlockSpec((1, 1, tq, QK_DIM), lambda b, h, i: (b, h, i, 0))]
                 + [kspec(a) for a in ks] + [kspec(a) for a in vs],
        out_specs=pl.BlockSpec((tq, HEAD_DIM), lambda b, h, i: (b * nq + i, h)),
        out_shape=jax.ShapeDtypeStruct((B * Tq, Hm * HEAD_DIM), BF16),
        compiler_params=_params("parallel", "parallel", "parallel"),
        name=f"attention_{nseg}seg",
    )(q, *ks, *vs)


def _merge_kernel(ac_ref, ag_ref, am_ref, wc_ref, wg_ref, wm_ref, gc_ref, gg_ref, gm_ref, o_ref):
    sig = lambda r: jax.nn.sigmoid(r[...].astype(F32))
    m = (sig(gc_ref) * _dot(ac_ref[...], wc_ref[...]) + sig(gg_ref) * _dot(ag_ref[...], wg_ref[...])
         + sig(gm_ref) * _dot(am_ref[...], wm_ref[...]))
    o_ref[...] = m.astype(o_ref.dtype)


def merge_branches(ac, ag, am, wc, wg, wm, u, off):
    N = ac.shape[0]
    D = wc.shape[1]
    tm = _pick(N, (1024, 512, 256, 128))
    tn = _pick((D, off["gate_conv"], off["gate_gdn"], off["gate_mla"]), (512, 256, 128))
    act = lambda a: pl.BlockSpec((tm, a.shape[1]), lambda i, j: (i, 0))
    wsp = lambda w: pl.BlockSpec((w.shape[0], tn), lambda i, j: (0, j))
    gate = lambda name: pl.BlockSpec((tm, tn), lambda i, j: (i, off[name] // tn + j))
    return pl.pallas_call(
        _merge_kernel,
        grid=(N // tm, D // tn),
        in_specs=[act(ac), act(ag), act(am), wsp(wc), wsp(wg), wsp(wm),
                  gate("gate_conv"), gate("gate_gdn"), gate("gate_mla")],
        out_specs=pl.BlockSpec((tm, tn), lambda i, j: (i, j)),
        out_shape=jax.ShapeDtypeStruct((N, D), BF16),
        compiler_params=_params("parallel", "parallel"),
        name="merge_branches",
    )(ac, ag, am, wc, wg, wm, u, u, u)


def _out_proj_kernel(m_ref, w_ref, h_ref, g_ref, o_ref):
    o_ref[...] = h_ref[...] + g_ref[0] * _dot(m_ref[...], w_ref[...])


def out_projection(m, w_o, h, gate, rows_per_group):
    N, D = h.shape
    tm = _pick(rows_per_group, (1024, 512, 256, 128))
    tn = _pick(D, (512, 256, 128))
    gdiv = rows_per_group // tm
    return pl.pallas_call(
        _out_proj_kernel,
        grid=(N // tm, D // tn),
        in_specs=[pl.BlockSpec((tm, m.shape[1]), lambda i, j: (i, 0)),
                  pl.BlockSpec((m.shape[1], tn), lambda i, j: (0, j)),
                  pl.BlockSpec((tm, tn), lambda i, j: (i, j)),
                  pl.BlockSpec((1, 1, tn), lambda i, j: (i // gdiv, 0, j))],
        out_specs=pl.BlockSpec((tm, tn), lambda i, j: (i, j)),
        out_shape=jax.ShapeDtypeStruct((N, D), F32),
        input_output_aliases={2: 0},
        compiler_params=_params("parallel", "parallel"),
        name="out_projection",
    )(m, w_o, h, gate)


def _router_kernel(x_ref, g_ref, sh_ref, sc_ref, r_ref, hn_ref, lg_ref):
    a = _norm_mod(x_ref[...], g_ref[...], sh_ref[0], sc_ref[0]).astype(BF16)
    hn_ref[...] = a
    lg_ref[0] = _dot_nt(r_ref[...], a)


def router(x, g, sh, sc, router_t, B, T, rows_per_group):
    N, D = x.shape
    E = router_t.shape[0]
    tm = _pick(T, (1024, 512, 256, 128))
    nt = T // tm
    gdiv = rows_per_group // tm
    return pl.pallas_call(
        _router_kernel,
        grid=(N // tm,),
        in_specs=[pl.BlockSpec((tm, D), lambda i: (i, 0)),
                  pl.BlockSpec((1, D), lambda i: (0, 0)),
                  pl.BlockSpec((1, 1, D), lambda i: (i // gdiv, 0, 0)),
                  pl.BlockSpec((1, 1, D), lambda i: (i // gdiv, 0, 0)),
                  pl.BlockSpec((E, D), lambda i: (0, 0))],
        out_specs=[pl.BlockSpec((tm, D), lambda i: (i, 0)),
                   pl.BlockSpec((1, E, tm), lambda i: (i // nt, 0, i % nt))],
        out_shape=[jax.ShapeDtypeStruct((N, D), BF16), jax.ShapeDtypeStruct((B, E, T), F32)],
        compiler_params=_params("parallel"),
        name="router",
    )(x, g.reshape(1, D), sh, sc, router_t)


def _route_kernel(lg_ref, rm_ref, aff_ref, rmt_ref, *, T, E, cap, W):
    lg = lg_ref[0]
    mx = jnp.max(lg, axis=0, keepdims=True)
    ex = jnp.exp(lg - mx)
    aff = ex / jnp.sum(ex, axis=0, keepdims=True)
    aff_ref[0] = aff
    bits = lax.bitcast_convert_type(aff, jnp.int32)

    def count(mask):
        return jnp.sum(jnp.where(mask, 1.0, 0.0), axis=1, keepdims=True)

    thr = jnp.zeros((E, 1), jnp.int32)
    for bit in range(30, -1, -1):
        cand = thr | (1 << bit)
        thr = jnp.where(count(bits >= cand) >= cap, cand, thr)
    gt = bits > thr
    eq = bits == thr
    need = cap - count(gt)

    ri = lax.broadcasted_iota(jnp.int32, (W, W), 0)
    ci = lax.broadcasted_iota(jnp.int32, (W, W), 1)
    upper = jnp.where(ri < ci, 1.0, 0.0).astype(BF16)
    ident = jnp.where(ri == ci, 1.0, 0.0).astype(BF16)

    def excl_cumsum(mask):
        m = jnp.where(mask, 1.0, 0.0)
        outs, carry = [], jnp.zeros((E, 1), F32)
        for j in range(T // W):
            blk = m[:, j * W:(j + 1) * W]
            outs.append(_dot(blk.astype(BF16), upper) + carry)
            carry = carry + jnp.sum(blk, axis=1, keepdims=True)
        return jnp.concatenate(outs, axis=1) if len(outs) > 1 else outs[0]

    sel = gt | (eq & (excl_cumsum(eq) < need))
    rm = jnp.where(sel, excl_cumsum(sel), -1.0)
    rm_ref[0] = rm
    rmb = rm.astype(BF16)
    for j in range(T // W):
        rmt_ref[0, j * W:(j + 1) * W, :] = _dot_nt(ident, rmb[:, j * W:(j + 1) * W])


def route(logits_t, cap):
    B, E, T = logits_t.shape
    W = 256 if T % 256 == 0 else 128
    blk = pl.BlockSpec((1, E, T), lambda b: (b, 0, 0))
    return pl.pallas_call(
        functools.partial(_route_kernel, T=T, E=E, cap=cap, W=W),
        grid=(B,),
        in_specs=[blk],
        out_specs=[blk, blk, pl.BlockSpec((1, T, E), lambda b: (b, 0, 0))],
        out_shape=[jax.ShapeDtypeStruct((B, E, T), F32), jax.ShapeDtypeStruct((B, E, T), F32),
                   jax.ShapeDtypeStruct((B, T, E), F32)],
        compiler_params=_params("parallel"),
        name="route",
    )(logits_t)


def _gather_kernel(rm_ref, aff_ref, hn_ref, xs_ref, gate_ref, *, cap):
    rm = rm_ref[0, 0]
    T = rm.shape[1]
    slot = lax.broadcasted_iota(jnp.int32, (cap, T), 0).astype(F32)
    hit = slot == rm
    xs_ref[0, 0] = _dot(jnp.where(hit, 1.0, 0.0).astype(BF16), hn_ref[...]).astype(xs_ref.dtype)
    gate_ref[0, 0] = jnp.sum(jnp.where(hit, aff_ref[0, 0], 0.0), axis=1, keepdims=True)


def moe_gather(rm, aff, hn, cap):
    B, E, T = rm.shape
    D = hn.shape[1]
    rowspec = pl.BlockSpec((1, 1, 1, T), lambda b, e: (b, e, 0, 0))
    return pl.pallas_call(
        functools.partial(_gather_kernel, cap=cap),
        grid=(B, E),
        in_specs=[rowspec, rowspec, pl.BlockSpec((T, D), lambda b, e: (b, 0))],
        out_specs=[pl.BlockSpec((1, 1, cap, D), lambda b, e: (b, e, 0, 0)),
                   pl.BlockSpec((1, 1, cap, 1), lambda b, e: (b, e, 0, 0))],
        out_shape=[jax.ShapeDtypeStruct((B, E, cap, D), BF16), jax.ShapeDtypeStruct((B, E, cap, 1), F32)],
        compiler_params=_params("parallel", "parallel"),
        name="moe_gather",
    )(rm.reshape(B, E, 1, T), aff.reshape(B, E, 1, T), hn)


def _ffn_kernel(xs_ref, gate_ref, wg_ref, wu_ref, wd_ref, ye_ref):
    bb, _, cap, D = xs_ref.shape
    xs = xs_ref[...].reshape(bb * cap, D)
    hid = (_silu(_dot(xs, wg_ref[0])) * _dot(xs, wu_ref[0])).astype(BF16)
    ye = _dot(hid, wd_ref[0]) * gate_ref[...].reshape(bb * cap, 1)
    ye_ref[...] = ye.reshape(bb, 1, cap, D).astype(ye_ref.dtype)


def moe_ffn(xs, gate, w_gate, w_up, w_down):
    B, E, cap, D = xs.shape
    FF = w_gate.shape[2]
    bb = max(1, min(B, 512 // cap))
    while B % bb:
        bb -= 1
    tok = lambda last: pl.BlockSpec((bb, 1, cap, last), lambda e, b: (b, e, 0, 0))
    return pl.pallas_call(
        _ffn_kernel,
        grid=(E, B // bb),
        in_specs=[tok(D), tok(1),
                  pl.BlockSpec((1, D, FF), lambda e, b: (e, 0, 0)),
                  pl.BlockSpec((1, D, FF), lambda e, b: (e, 0, 0)),
                  pl.BlockSpec((1, FF, D), lambda e, b: (e, 0, 0))],
        out_specs=tok(D),
        out_shape=jax.ShapeDtypeStruct((B, E, cap, D), BF16),
        compiler_params=_params("parallel", "parallel"),
        name="moe_ffn",
    )(xs, gate, w_gate, w_up, w_down)


def _scatter_kernel(rmt_ref, ye_ref, h_ref, g_ref, o_ref, *, E, cap):
    tt = rmt_ref.shape[1]
    rmt = rmt_ref[0]
    slot = lax.broadcasted_iota(jnp.int32, (tt, cap), 1).astype(F32)
    acc = jnp.zeros(o_ref.shape, F32)
    for e in range(E):
        onehot = jnp.where(rmt[:, e:e + 1] == slot, 1.0, 0.0).astype(BF16)
        acc = acc + _dot(onehot, ye_ref[0, e * cap:(e + 1) * cap, :])
    o_ref[...] = h_ref[...] + g_ref[0] * acc


def moe_scatter(rmt, ye, h, gate, T):
    B, _, E = rmt.shape
    _, EC, D = ye.shape
    cap = EC // E
    tt = _pick(T, (512, 256, 128))
    tn = _pick(D, (1024, 512, 256, 128))
    nt = T // tt
    return pl.pallas_call(
        functools.partial(_scatter_kernel, E=E, cap=cap),
        grid=(B, D // tn, nt),
        in_specs=[pl.BlockSpec((1, tt, E), lambda b, j, i: (b, i, 0)),
                  pl.BlockSpec((1, EC, tn), lambda b, j, i: (b, 0, j)),
                  pl.BlockSpec((tt, tn), lambda b, j, i: (b * nt + i, j)),
                  pl.BlockSpec((1, 1, tn), lambda b, j, i: (b, 0, j))],
        out_specs=pl.BlockSpec((tt, tn), lambda b, j, i: (b * nt + i, j)),
        out_shape=jax.ShapeDtypeStruct(h.shape, F32),
        input_output_aliases={2: 0},
        compiler_params=_params("parallel", "parallel", "parallel"),
        name="moe_scatter",
    )(rmt, ye, h, gate)


def expert_choice_moe(h, g, sh, sc, gate, router_t, w_gate, w_up, w_down, B, T, rows_per_group):
    E = router_t.shape[0]
    cap = EC_CAPACITY * T // E
    assert cap <= 256 and cap % 16 == 0, "slot indices must stay exact in bf16 and fill bf16 sublane tiles"
    hn, logits_t = router(h, g, sh, sc, router_t, B, T, rows_per_group)
    rm, aff, rmt = route(logits_t, cap)
    xs, gsl = moe_gather(rm, aff, hn, cap)
    ye = moe_ffn(xs, gsl, w_gate, w_up, w_down)
    return moe_scatter(rmt, ye.reshape(B, E * cap, ye.shape[-1]), h, gate, T)


def _final_norm_kernel(x_ref, g_ref, o_ref):
    o_ref[...] = _rms(x_ref[...], g_ref[...])


def final_norm(x, g):
    N, D = x.shape
    tm = _pick(N, (512, 256, 128))
    return pl.pallas_call(
        _final_norm_kernel,
        grid=(N // tm,),
        in_specs=[pl.BlockSpec((tm, D), lambda i: (i, 0)), pl.BlockSpec((1, D), lambda i: (0, 0))],
        out_specs=pl.BlockSpec((tm, D), lambda i: (i, 0)),
        out_shape=jax.ShapeDtypeStruct((N, D), F32),
        compiler_params=_params("parallel"),
        name="final_norm",
    )(x, g.reshape(1, D))


def _swap_rope(w):
    half = ROPE_DIM // 2
    return jnp.concatenate([-w[..., half:], w[..., :half]], axis=-1)


_PACKED_ORDER = ("mla_cq", "gdn_small", "mla_kpe", "mla_ckv", "conv_x", "conv_b", "conv_c", "gdn_q", "gdn_k", "gdn_v",
                 "gdn_z", "gate_conv", "gate_gdn", "gate_mla")


def _in_layout(C, Hg, RQ, RKV, D):
    width = {"mla_cq": RQ, "gdn_small": LANE, "mla_kpe": LANE, "mla_ckv": RKV, "conv_x": C, "conv_b": C, "conv_c": C,
             "gdn_q": Hg * HEAD_DIM, "gdn_k": Hg * HEAD_DIM, "gdn_v": Hg * HEAD_DIM, "gdn_z": Hg * HEAD_DIM,
             "gate_conv": D, "gate_gdn": D, "gate_mla": D}
    off, o = {}, 0
    for name in _PACKED_ORDER:
        off[name] = o
        o += width[name]
    assert off["mla_cq"] % RQ == 0 and off["mla_ckv"] % RKV == 0
    return off, o


def _pack_w_in(w, C, Hg, RQ, RKV, D):
    names = ("conv_x", "conv_b", "conv_c", "gdn_q", "gdn_k", "gdn_v", "gdn_z", "gdn_small", "mla_cq", "mla_ckv",
             "mla_kpe", "gate_conv", "gate_gdn", "gate_mla")
    sizes = (C, C, C, Hg * HEAD_DIM, Hg * HEAD_DIM, Hg * HEAD_DIM, Hg * HEAD_DIM, 4 * Hg, RQ, RKV, ROPE_DIM, D, D, D)
    seg, o = {}, 0
    for name, size in zip(names, sizes):
        seg[name] = w[:, o:o + size]
        o += size
    seg["gdn_small"] = jnp.pad(seg["gdn_small"], ((0, 0), (0, LANE - 4 * Hg)))
    seg["mla_kpe"] = jnp.concatenate([seg["mla_kpe"], _swap_rope(seg["mla_kpe"])], axis=1)
    return jnp.concatenate([seg[n] for n in _PACKED_ORDER], axis=1).astype(BF16)


def _rope_tables(T):
    rows = T // GRID_W
    row = jnp.repeat(jnp.arange(rows), GRID_W)
    col = jnp.tile(jnp.arange(GRID_W), rows)
    n_freq = ROPE_DIM // 4
    inv_freq = 1.0 / (ROPE_THETA ** (jnp.arange(n_freq, dtype=F32) / n_freq))
    ang = jnp.concatenate([row[:, None] * inv_freq, col[:, None] * inv_freq], axis=-1)
    cos, sin = jnp.cos(ang), jnp.sin(ang)
    return jnp.concatenate([cos, cos], axis=1), jnp.concatenate([sin, sin], axis=1)


def kernel(x, c, ctx, c_ctx, w_ada, b_ada, norm1_g, w_in, conv_w, conv_out, gdn_conv_w, gdn_a_log, gdn_dt_bias,
           gdn_norm_g, gdn_out, mla_q_norm_g, mla_w_uq, mla_kv_norm_g, mla_w_ukv, mla_out, w_o, norm2_g, router_w,
           w_gate, w_up, w_down, final_g):
    B, T, D = x.shape
    TC = ctx.shape[1]
    L = w_ada.shape[0]
    C = conv_w.shape[-1]
    Hg = gdn_a_log.shape[-1]
    RQ = mla_q_norm_g.shape[-1]
    RKV = mla_kv_norm_g.shape[-1]
    Hm = mla_w_uq.shape[-1] // QK_DIM
    assert 4 * Hg <= LANE and T % GDN_CHUNK == 0 and TC % GDN_CHUNK == 0

    off, _ = _in_layout(C, Hg, RQ, RKV, D)

    R = -(-(B + 1) // 8) * 8
    cc = jnp.concatenate([c, c_ctx[None, :], jnp.zeros((R - B - 1, D), F32)], axis=0)
    mod = ada_modulation(cc, w_ada, b_ada)

    cos64, sin64 = _rope_tables(T)
    q_ct, q_st = jnp.tile(cos64, (1, Hm)), jnp.tile(sin64, (1, Hm))
    q_ct0, q_st0 = jnp.ones((TC, Hm * ROPE_DIM), F32), jnp.zeros((TC, Hm * ROPE_DIM), F32)
    k_tab = jnp.concatenate([cos64, sin64], 1)
    k_tab0 = jnp.concatenate([jnp.ones((TC, ROPE_DIM), F32), jnp.zeros((TC, ROPE_DIM), F32)], 1)

    h = x.reshape(B * T, D)
    hc = ctx.reshape(B * TC, D)
    zero_state = jnp.zeros((B, Hg, HEAD_DIM, HEAD_DIM), F32)

    for l in range(L):
        need_ctx = l < L - 1
        w_in_l = _pack_w_in(w_in[l], C, Hg, RQ, RKV, D)
        wq = mla_w_uq[l].reshape(RQ, Hm, QK_DIM)
        wq_pe = wq[..., HEAD_DIM:]
        wq_main = jnp.concatenate([wq[..., :HEAD_DIM].reshape(RQ, -1), wq_pe.reshape(RQ, -1)], 1).astype(BF16)
        wq_aux = _swap_rope(wq_pe).reshape(RQ, -1).astype(BF16)
        wkv = mla_w_ukv[l].reshape(RKV, Hm, 2 * HEAD_DIM)
        w_kv = jnp.concatenate([wkv[..., :HEAD_DIM].reshape(RKV, -1), wkv[..., HEAD_DIM:].reshape(RKV, -1)],
                               1).astype(BF16)
        pad = jnp.zeros((LANE - 2 * Hg,), F32)
        alog_row = jnp.concatenate([gdn_a_log[l].reshape(-1), pad]).reshape(1, LANE)
        dtb_row = jnp.concatenate([gdn_dt_bias[l].reshape(-1), pad]).reshape(1, LANE)
        wc_o, wg_o, wm_o, wo_l = (a[l].astype(BF16) for a in (conv_out, gdn_out, mla_out, w_o))
        router_t = router_w[l].T.astype(BF16)
        wgt, wup, wdn = w_gate[l].astype(BF16), w_up[l].astype(BF16), w_down[l].astype(BF16)

        ml = mod[l, :B].reshape(B, 1, 6 * D)
        mc = mod[l, B:B + 1].reshape(1, 1, 6 * D)
        sh1, sc1, g1, sh2, sc2, g2 = (ml[..., i * D:(i + 1) * D] for i in range(6))
        csh1, csc1, cg1, csh2, csc2, cg2 = (mc[..., i * D:(i + 1) * D] for i in range(6))

        u_lat = in_projection(h, norm1_g[l], sh1, sc1, w_in_l, T)
        u_ctx = in_projection(hc, norm1_g[l], csh1, csc1, w_in_l, B * TC)

        ag_ctx, s_f, s_b = gdn_mixer(u_ctx, gdn_conv_w[l], alog_row, dtb_row, gdn_norm_g[l], zero_state, zero_state,
                                     B, TC, Hg, off)
        ag_lat, _, _ = gdn_mixer(u_lat, gdn_conv_w[l], alog_row, dtb_row, gdn_norm_g[l], s_f, s_b, B, T, Hg, off)

        k_ctx, v_ctx = mla_keys_values(u_ctx, mla_kv_norm_g[l], w_kv, k_tab0, B, TC, Hm, off)
        k_lat, v_lat = mla_keys_values(u_lat, mla_kv_norm_g[l], w_kv, k_tab, B, T, Hm, off)
        q_lat = mla_queries(u_lat, mla_q_norm_g[l], wq_main, wq_aux, q_ct, q_st, B, T, Hm, off)
        am_lat = attention(q_lat, [k_ctx, k_lat], [v_ctx, v_lat])

        ac_lat = conv_mixer_front(u_lat, conv_w[l], B, T, off)
        m_lat = merge_branches(ac_lat, ag_lat, am_lat, wc_o, wg_o, wm_o, u_lat, off)
        h = out_projection(m_lat, wo_l, h, g1, T)
        h = expert_choice_moe(h, norm2_g[l], sh2, sc2, g2, router_t, wgt, wup, wdn, B, T, T)

        if need_ctx:
            q_ctx = mla_queries(u_ctx, mla_q_norm_g[l], wq_main, wq_aux, q_ct0, q_st0, B, TC, Hm, off)
            am_ctx = attention(q_ctx, [k_ctx], [v_ctx])
            ac_ctx = conv_mixer_front(u_ctx, conv_w[l], B, TC, off)
            m_ctx = merge_branches(ac_ctx, ag_ctx, am_ctx, wc_o, wg_o, wm_o, u_ctx, off)
            hc = out_projection(m_ctx, wo_l, hc, cg1, B * TC)
            cg2b = jnp.broadcast_to(cg2, (B, 1, D))
            hc = expert_choice_moe(hc, norm2_g[l], csh2, csc2, cg2b, router_t, wgt, wup, wdn, B, TC, B * TC)

    return final_norm(h, final_g).reshape(B, T, D)
```

```python
import functools
import math

import jax
import jax.numpy as jnp
from jax import lax
from jax.experimental import pallas as pl
from jax.experimental.pallas import tpu as pltpu

F32 = jnp.float32
BF16 = jnp.bfloat16

RMS_EPS = 1e-6
L2_EPS = 1e-6
ROPE_THETA = 10000.0
GRID_W = 64
HEAD_DIM = 128
ROPE_DIM = 64
QK_DIM = HEAD_DIM + ROPE_DIM
GDN_CHUNK = 64
GDN_SUB = 16
EC_CAPACITY = 2
LANE = 128
V_PAD = 16
ATTN_LOGIT_SCALE = QK_DIM ** -0.5 * math.log2(math.e)
VMEM_LIMIT = 56 * 1024 * 1024


def _pick(n, cands):
    ns = n if isinstance(n, tuple) else (n,)
    for c in cands:
        if all(v % c == 0 for v in ns):
            return c
    raise ValueError(f"no tile in {cands} divides {ns}")


def _params(*sem):
    return pltpu.CompilerParams(dimension_semantics=sem, vmem_limit_bytes=VMEM_LIMIT)


def _dot(a, b):
    return jnp.dot(a, b, preferred_element_type=F32)


def _dot_nt(a, b):
    return lax.dot_general(a, b, (((1,), (1,)), ((), ())), preferred_element_type=F32)


def _dot_tn(a, b):
    return lax.dot_general(a, b, (((0,), (0,)), ((), ())), preferred_element_type=F32)


def _split2(x):
    hi = x.astype(BF16)
    lo = (x - hi.astype(F32)).astype(BF16)
    return hi, lo


def _split3(x):
    hi = x.astype(BF16)
    r = x - hi.astype(F32)
    mid = r.astype(BF16)
    lo = (r - mid.astype(F32)).astype(BF16)
    return hi, mid, lo


def _silu(x):
    return x * jax.nn.sigmoid(x)


def _softplus(x):
    return jnp.maximum(x, 0.0) + jnp.log(1.0 + jnp.exp(-jnp.abs(x)))


def _ada_kernel(c_ref, w_ref, b_ref, o_ref):
    a = _silu(c_ref[...])
    ah, al = _split2(a)
    wh, wl = _split2(w_ref[0])
    o_ref[0] = _dot(ah, wh) + (_dot(ah, wl) + _dot(al, wh)) + b_ref[0]


def ada_modulation(cc, w_ada, b_ada):
    L, D, D6 = w_ada.shape
    R = cc.shape[0]
    tn = _pick(D6, (1024, 512, 256, 128))
    return pl.pallas_call(
        _ada_kernel,
        grid=(L, D6 // tn),
        in_specs=[pl.BlockSpec((R, D), lambda l, j: (0, 0)),
                  pl.BlockSpec((1, D, tn), lambda l, j: (l, 0, j)),
                  pl.BlockSpec((1, 1, tn), lambda l, j: (l, 0, j))],
        out_specs=pl.BlockSpec((1, R, tn), lambda l, j: (l, 0, j)),
        out_shape=jax.ShapeDtypeStruct((L, R, D6), F32),
        compiler_params=_params("parallel", "parallel"),
        name="ada_modulation",
    )(cc, w_ada, b_ada.reshape(L, 1, D6))


def _norm_mod(x, g, sh, sc):
    ms = jnp.mean(x * x, axis=-1, keepdims=True)
    y = x * lax.rsqrt(ms + RMS_EPS) * g
    return y * (1.0 + sc) + sh


def _in_proj_kernel(x_ref, g_ref, sh_ref, sc_ref, w_ref, o_ref, a_scr):
    @pl.when(pl.program_id(1) == 0)
    def _():
        a_scr[...] = _norm_mod(x_ref[...], g_ref[...], sh_ref[0], sc_ref[0]).astype(BF16)

    o_ref[...] = _dot(a_scr[...], w_ref[0]).astype(o_ref.dtype)


def in_projection(x, g, sh, sc, w, l, rows_per_group):
    N, D = x.shape
    NC = w.shape[2]
    tm = _pick(rows_per_group, (1024, 512, 256, 128))
    tn = _pick(NC, (512, 256, 128))
    gdiv = rows_per_group // tm
    return pl.pallas_call(
        _in_proj_kernel,
        grid=(N // tm, NC // tn),
        in_specs=[pl.BlockSpec((tm, D), lambda i, j: (i, 0)),
                  pl.BlockSpec((1, D), lambda i, j: (0, 0)),
                  pl.BlockSpec((1, 1, D), lambda i, j: (i // gdiv, 0, 0)),
                  pl.BlockSpec((1, 1, D), lambda i, j: (i // gdiv, 0, 0)),
                  pl.BlockSpec((1, D, tn), lambda i, j: (l, 0, j))],
        out_specs=pl.BlockSpec((tm, tn), lambda i, j: (i, j)),
        out_shape=jax.ShapeDtypeStruct((N, NC), BF16),
        scratch_shapes=[pltpu.VMEM((tm, D), BF16)],
        compiler_params=_params("parallel", "arbitrary"),
        name="in_projection",
    )(x, g.reshape(1, D), sh, sc, w)


def _conv3(x, w, T):
    row = lax.broadcasted_iota(jnp.int32, x.shape, 0)
    prev = jnp.where(row == 0, 0.0, pltpu.roll(x, 1, 0))
    nxt = jnp.where(row == T - 1, 0.0, pltpu.roll(x, T - 1, 0))
    return prev * w[0:1] + x * w[1:2] + nxt * w[2:3]


def _conv_mixer_kernel(x_ref, b_ref, c_ref, w_ref, o_ref, *, T):
    v = c_ref[...].astype(F32) * x_ref[...].astype(F32)
    o_ref[...] = (b_ref[...].astype(F32) * _conv3(v, w_ref[...], T)).astype(o_ref.dtype)


def conv_mixer_front(u, conv_w, B, T, off):
    C = conv_w.shape[1]
    tc = _pick(C, (256, 128))
    ox, ob, oc = (off[k] // tc for k in ("conv_x", "conv_b", "conv_c"))
    return pl.pallas_call(
        functools.partial(_conv_mixer_kernel, T=T),
        grid=(B, C // tc),
        in_specs=[pl.BlockSpec((T, tc), lambda b, j: (b, ox + j)),
                  pl.BlockSpec((T, tc), lambda b, j: (b, ob + j)),
                  pl.BlockSpec((T, tc), lambda b, j: (b, oc + j)),
                  pl.BlockSpec((3, tc), lambda b, j: (0, j))],
        out_specs=pl.BlockSpec((T, tc), lambda b, j: (b, j)),
        out_shape=jax.ShapeDtypeStruct((B * T, C), BF16),
        compiler_params=_params("parallel", "parallel"),
        name="conv_mixer_front",
    )(u, u, u, conv_w)


def _gdn_solve(a_list, rhs_list, sub_mask, eye):
    g = a_list[0].shape[0]
    b16 = lambda ts: [t.astype(BF16) for t in ts]
    dots = lambda xs, ys: [_dot(x, y) for x, y in zip(xs, ys)]
    d = [jnp.where(sub_mask, a, 0.0) for a in a_list]
    n = [a - dd for a, dd in zip(a_list, d)]
    db = b16(d)
    d2b = b16(dots(db, db))
    d4b = b16(dots(d2b, d2b))
    d8b = b16(dots(d4b, d4b))
    x = [eye - dd for dd in d]
    for p in (d2b, d4b, d8b):
        x = [xx + t for xx, t in zip(x, dots(b16(x), p))]
    z = dots(b16(x), b16([jnp.concatenate([nn, r], axis=1) for nn, r in zip(n, rhs_list)]))
    mb = b16([zz[:, :g] for zz in z])
    zr = [zz[:, g:] for zz in z]
    m2b = b16(dots(mb, mb))
    zr = [r + t for r, t in zip(zr, dots(m2b, b16(zr)))]
    return [r - t for r, t in zip(zr, dots(mb, b16(zr)))]


def _gdn_kernel(q_ref, k_ref, v_ref, z_ref, ab_ref, wq_ref, wk_ref, wv_ref, alog_ref, dtb_ref, ng_ref,
                s0f_ref, s0b_ref, o_ref, sf_ref, sb_ref,
                q_s, k_s, v_s, col_s, u_s, w_s, qg_s, kd_s, in_s, eg_s, o_s, st_s, *, T, H, G, GI, HB):
    hb = pl.program_id(1)
    NC = T // GDN_CHUNK
    L = GDN_CHUNK

    def l2n(x):
        return x * lax.rsqrt(jnp.sum(x * x, axis=-1, keepdims=True) + L2_EPS)

    ri = lax.broadcasted_iota(jnp.int32, (G, G), 0)
    ci = lax.broadcasted_iota(jnp.int32, (G, G), 1)
    same_chunk = (ri >> int(math.log2(L))) == (ci >> int(math.log2(L)))
    same_sub = (ri >> int(math.log2(GDN_SUB))) == (ci >> int(math.log2(GDN_SUB)))
    eye = jnp.where(ri == ci, 1.0, 0.0).astype(F32)
    ones_chunk = jnp.where(same_chunk, 1.0, 0.0).astype(BF16)

    ab = ab_ref[...].astype(F32)
    lane = lax.broadcasted_iota(jnp.int32, ab.shape, 1)
    gfull = -jnp.exp(alog_ref[...]) * _softplus(ab + dtb_ref[...])
    bfull = jax.nn.sigmoid(ab)

    def col(x, idx):
        return jnp.sum(jnp.where(lane == idx, x, 0.0), axis=1, keepdims=True)

    for hh in range(HB):
        hsl = slice(hh * HEAD_DIM, (hh + 1) * HEAD_DIM)
        head = hb * HB + hh

        def prep(x_ref, w_ref):
            return _silu(_conv3(x_ref[:, hsl].astype(F32), w_ref[:, hsl], T))

        q_s[hh] = l2n(prep(q_ref, wq_ref)) * (HEAD_DIM ** -0.5)
        k_s[hh] = l2n(prep(k_ref, wk_ref))
        v_s[hh] = prep(v_ref, wv_ref)

        g_f, g_b = col(gfull, head), col(gfull, H + head)
        b_f, b_b = col(bfull, 2 * H + head), col(bfull, 3 * H + head)
        col_s[hh] = jnp.where(lane == 0, g_f, jnp.where(lane == 1, g_b, jnp.where(lane == 2, b_f, b_b)))
        st_s[hh, 0] = s0f_ref[0, hh]
        st_s[hh, 1] = s0b_ref[0, hh]

    chains = [(hh, d) for hh in range(HB) for d in range(2)]
    incl = [same_chunk & (ci <= ri), same_chunk & (ci >= ri)]
    strict = [same_chunk & (ci < ri), same_chunk & (ci > ri)]
    incl_b = [jnp.where(m, 1.0, 0.0).astype(BF16) for m in incl]
    ones_8g = jnp.ones((8, G), BF16)

    def exact3_list(ms, xs):
        parts = [_split3(x) for x in xs]
        return [_dot(m, p[0]) + _dot(m, p[1]) + _dot(m, p[2]) for m, p in zip(ms, parts)]

    def group_body(gi, carry):
        units = [(j, hh) for j in range(GI) for hh in range(HB)]
        gch = [(ui, d) for ui in range(len(units)) for d in range(2)]
        r0 = [pl.multiple_of((gi * GI + j) * G, G) for j in range(GI)]
        rows = [pl.ds(r0[j], G) for j, _ in units]
        qg = [q_s[hh, r, :] for (_, hh), r in zip(units, rows)]
        kg = [k_s[hh, r, :] for (_, hh), r in zip(units, rows)]
        vg = [v_s[hh, r, :] for (_, hh), r in zip(units, rows)]
        cols = [col_s[hh, r, :] for (_, hh), r in zip(units, rows)]
        kb16 = [k.astype(BF16) for k in kg]
        kk = [_dot_nt(kb, kb) for kb in kb16]
        qk = [_dot_nt(q.astype(BF16), kb) for q, kb in zip(qg, kb16)]
        bcol = [cols[ui][:, 2 + d:3 + d] for ui, d in gch]
        gb = [jnp.broadcast_to(cols[ui][:, d:d + 1], (G, LANE)) for ui, d in gch]
        gc = exact3_list([incl_b[d] for _, d in gch], gb)
        gl = exact3_list([ones_chunk] * len(gch), gb)
        cmat = [jnp.concatenate([g] * (G // LANE), axis=1) if G > LANE else g for g in gc]
        rrow = [r[0:1] for r in exact3_list([ones_8g] * len(gch),
                                            [jnp.where(ri == ci, cm, 0.0) for cm in cmat])]
        decay = [jnp.where(incl[d], jnp.exp(jnp.where(incl[d], cm - rr, 0.0)), 0.0)
                 for (_, d), cm, rr in zip(gch, cmat, rrow)]
        a = [jnp.where(strict[d], bc * kk[ui] * dc, 0.0) for (ui, d), bc, dc in zip(gch, bcol, decay)]
        egc = [jnp.exp(g) for g in gc]
        rhs = [jnp.concatenate([vg[ui] * bc, kg[ui] * bc * e], axis=1) for (ui, _), bc, e in zip(gch, bcol, egc)]
        uw = _gdn_solve(a, rhs, same_sub, eye)
        for i, (ui, d) in enumerate(gch):
            j, hh = units[ui]
            u_s[hh, d, rows[ui], :] = uw[i][:, :HEAD_DIM]
            w_s[hh, d, rows[ui], :] = uw[i][:, HEAD_DIM:].astype(BF16)
            qg_s[hh, d, rows[ui], :] = (qg[ui] * egc[i]).astype(BF16)
            kd_s[hh, d, rows[ui], :] = (kg[ui] * jnp.exp(gl[i] - gc[i])).astype(BF16)
            egl = jnp.exp(gl[i])
            intra = (qk[ui] * decay[i]).astype(BF16)
            for c in range(G // L):
                e0 = pl.multiple_of((gi * GI + j) * (G // 8) + c * 8, 8)
                eg_s[hh, d, pl.ds(e0, 8), :] = egl[c * L:c * L + 8]
                in_s[hh, d, pl.ds(r0[j] + c * L, L), :] = intra[c * L:(c + 1) * L, c * L:(c + 1) * L]
        return carry

    lax.fori_loop(0, T // (G * GI), group_body, 0)

    def scan_body(c, carry):
        rws = [pl.ds(pl.multiple_of((c if d == 0 else NC - 1 - c) * L, L), L) for _, d in chains]
        egr = [pl.ds(pl.multiple_of((c if d == 0 else NC - 1 - c) * 8, 8), 8) for _, d in chains]
        s = [st_s[hh, d] for hh, d in chains]
        wq = [jnp.concatenate([w_s[hh, d, r, :], qg_s[hh, d, r, :]], axis=0) for (hh, d), r in zip(chains, rws)]
        ws = [_dot(x, y.astype(BF16)) for x, y in zip(wq, s)]
        vb = [(u_s[hh, d, r, :] - w[:L]).astype(BF16) for (hh, d), r, w in zip(chains, rws, ws)]
        oi = [_dot(in_s[hh, d, r, :], v) for (hh, d), r, v in zip(chains, rws, vb)]
        ds_ = [_dot_tn(kd_s[hh, d, r, :], v) for (hh, d), r, v in zip(chains, rws, vb)]
        for i, (hh, d) in enumerate(chains):
            o_s[hh, d, rws[i], :] = ws[i][L:] + oi[i]
            st_s[hh, d] = s[i] * eg_s[hh, d, egr[i], :][0:1] + ds_[i]
        return carry

    lax.fori_loop(0, NC, scan_body, 0)

    for hh in range(HB):
        hsl = slice(hh * HEAD_DIM, (hh + 1) * HEAD_DIM)
        sf_ref[0, hh] = st_s[hh, 0]
        sb_ref[0, hh] = st_s[hh, 1]
        o = o_s[hh, 0] + o_s[hh, 1]
        y = o * lax.rsqrt(jnp.mean(o * o, axis=-1, keepdims=True) + RMS_EPS) * ng_ref[...]
        o_ref[:, hsl] = (y * _silu(z_ref[:, hsl].astype(F32))).astype(o_ref.dtype)


def gdn_mixer(u, conv_w, alog_row, dtb_row, norm_g, s0f, s0b, B, T, H, off):
    G = 2 * GDN_CHUNK
    GI = 2 if T % (2 * G) == 0 else 1
    HB = 2 if H % 2 == 0 else 1
    W = HB * HEAD_DIM
    assert all(off[k] % W == 0 for k in ("gdn_q", "gdn_k", "gdn_v", "gdn_z"))
    oq, ok, ov, oz = (off[k] // W for k in ("gdn_q", "gdn_k", "gdn_v", "gdn_z"))
    oab = off["gdn_small"] // LANE
    seq = lambda o: pl.BlockSpec((T, W), lambda b, h: (b, o + h))
    cw = lambda o: pl.BlockSpec((3, W), lambda b, h: (0, o * (H // HB) + h))
    row = pl.BlockSpec((1, LANE), lambda b, h: (0, 0))
    st = pl.BlockSpec((1, HB, HEAD_DIM, HEAD_DIM), lambda b, h: (b, h, 0, 0))
    tbuf = lambda: pltpu.VMEM((HB, T, HEAD_DIM), F32)
    hbuf = lambda dt: pltpu.VMEM((HB, 2, T, HEAD_DIM), dt)
    return pl.pallas_call(
        functools.partial(_gdn_kernel, T=T, H=H, G=G, GI=GI, HB=HB),
        grid=(B, H // HB),
        in_specs=[seq(oq), seq(ok), seq(ov), seq(oz),
                  pl.BlockSpec((T, LANE), lambda b, h: (b, oab)),
                  cw(0), cw(1), cw(2), row, row, row, st, st],
        out_specs=[pl.BlockSpec((T, W), lambda b, h: (b, h)), st, st],
        out_shape=[jax.ShapeDtypeStruct((B * T, H * HEAD_DIM), BF16),
                   jax.ShapeDtypeStruct((B, H, HEAD_DIM, HEAD_DIM), F32),
                   jax.ShapeDtypeStruct((B, H, HEAD_DIM, HEAD_DIM), F32)],
        scratch_shapes=[tbuf(), tbuf(), tbuf(), tbuf(), hbuf(F32), hbuf(BF16), hbuf(BF16), hbuf(BF16),
                        pltpu.VMEM((HB, 2, T, GDN_CHUNK), BF16), pltpu.VMEM((HB, 2, T // 8, HEAD_DIM), F32),
                        hbuf(F32), pltpu.VMEM((HB, 2, HEAD_DIM, HEAD_DIM), F32)],
        compiler_params=_params("parallel", "parallel"),
        name="gdn_mixer",
    )(u, u, u, u, u, conv_w, conv_w, conv_w, alog_row, dtb_row, norm_g.reshape(1, HEAD_DIM), s0f, s0b)


def _rms(x, g):
    return x * lax.rsqrt(jnp.mean(x * x, axis=-1, keepdims=True) + RMS_EPS) * g


def _mla_q_kernel(x_ref, g_ref, wm_ref, wa_ref, ct_ref, st_ref, o_ref, *, Hm):
    a = _rms(x_ref[...].astype(F32), g_ref[...]).astype(BF16)
    main = _dot(a, wm_ref[...]) * ATTN_LOGIT_SCALE
    nn = Hm * HEAD_DIM
    pe = main[:, nn:] * ct_ref[...] + (_dot(a, wa_ref[...]) * ATTN_LOGIT_SCALE) * st_ref[...]
    for h in range(Hm):
        o_ref[0, h, :, 0:HEAD_DIM] = main[:, h * HEAD_DIM:(h + 1) * HEAD_DIM].astype(o_ref.dtype)
        o_ref[0, h, :, HEAD_DIM:QK_DIM] = pe[:, h * ROPE_DIM:(h + 1) * ROPE_DIM].astype(o_ref.dtype)


def mla_queries(u, norm_g, w_main, w_aux, ctab, stab, B, T, Hm, off):
    R = w_main.shape[0]
    tm = _pick(T, (512, 256, 128))
    nt = T // tm
    oc = off["mla_cq"] // R
    full = lambda a: pl.BlockSpec(a.shape, lambda i: (0, 0))
    tab = pl.BlockSpec((tm, Hm * ROPE_DIM), lambda i: (i % nt, 0))
    return pl.pallas_call(
        functools.partial(_mla_q_kernel, Hm=Hm),
        grid=(B * nt,),
        in_specs=[pl.BlockSpec((tm, R), lambda i: (i, oc)), pl.BlockSpec((1, R), lambda i: (0, 0)),
                  full(w_main), full(w_aux), tab, tab],
        out_specs=pl.BlockSpec((1, Hm, tm, QK_DIM), lambda i: (i // nt, 0, i % nt, 0)),
        out_shape=jax.ShapeDtypeStruct((B, Hm, T, QK_DIM), BF16),
        compiler_params=_params("parallel"),
        name="mla_queries",
    )(u, norm_g.reshape(1, R), w_main, w_aux, ctab, stab)


def _mla_kv_kernel(x_ref, pe_ref, g_ref, wk_ref, wvt_ref, tab_ref, k_ref, vt_ref, *, Hm):
    a = _rms(x_ref[...].astype(F32), g_ref[...]).astype(BF16)
    kn = _dot(a, wk_ref[...])
    r = pe_ref[...].astype(F32) * tab_ref[...]
    pe = (r + pltpu.roll(r, ROPE_DIM, 1))[:, 0:ROPE_DIM].astype(k_ref.dtype)
    for h in range(Hm):
        k_ref[0, h, :, 0:HEAD_DIM] = kn[:, h * HEAD_DIM:(h + 1) * HEAD_DIM].astype(k_ref.dtype)
        k_ref[0, h, :, HEAD_DIM:QK_DIM] = pe
        vt_ref[0, h, 0:HEAD_DIM, :] = _dot_nt(wvt_ref[h], a).astype(vt_ref.dtype)
        vt_ref[0, h, HEAD_DIM:, :] = jnp.ones((V_PAD, a.shape[0]), vt_ref.dtype)


def mla_keys_values(u, norm_g, w_k, w_vt, tab, B, T, Hm, off):
    R = w_k.shape[0]
    tm = _pick(T, (512, 256, 128))
    nt = T // tm
    oc, op = off["mla_ckv"] // R, off["mla_kpe"] // LANE
    return pl.pallas_call(
        functools.partial(_mla_kv_kernel, Hm=Hm),
        grid=(B * nt,),
        in_specs=[pl.BlockSpec((tm, R), lambda i: (i, oc)),
                  pl.BlockSpec((tm, LANE), lambda i: (i, op)),
                  pl.BlockSpec((1, R), lambda i: (0, 0)),
                  pl.BlockSpec(w_k.shape, lambda i: (0, 0)),
                  pl.BlockSpec(w_vt.shape, lambda i: (0, 0, 0)),
                  pl.BlockSpec((tm, LANE), lambda i: (i % nt, 0))],
        out_specs=[pl.BlockSpec((1, Hm, tm, QK_DIM), lambda i: (i // nt, 0, i % nt, 0)),
                   pl.BlockSpec((1, Hm, HEAD_DIM + V_PAD, tm), lambda i: (i // nt, 0, 0, i % nt))],
        out_shape=[jax.ShapeDtypeStruct((B, Hm, T, QK_DIM), BF16),
                   jax.ShapeDtypeStruct((B, Hm, HEAD_DIM + V_PAD, T), BF16)],
        compiler_params=_params("parallel"),
        name="mla_keys_values",
    )(u, u, norm_g.reshape(1, R), w_k, w_vt, tab)


def _attn_kernel(*refs, nseg, nsub):
    q_ref = refs[0]
    k_refs = refs[1:1 + nseg]
    vt_refs = refs[1 + nseg:1 + 2 * nseg]
    o_ref = refs[1 + 2 * nseg]
    rs = q_ref.shape[2] // nsub

    def scores(i):
        q = q_ref[0, 0, i * rs:(i + 1) * rs, :]
        return [_dot_nt(k[0, 0], q) for k in k_refs]

    def finish(i, s):
        m = functools.reduce(jnp.maximum, [jnp.max(x, axis=0, keepdims=True) for x in s])
        acc = functools.reduce(jnp.add, [_dot(vt[0, 0], jnp.exp2(x - m).astype(BF16)) for x, vt in zip(s, vt_refs)])
        o = acc[0:HEAD_DIM] / acc[HEAD_DIM:HEAD_DIM + 1]
        o_ref[i * rs:(i + 1) * rs, :] = o.T.astype(o_ref.dtype)

    s_next = scores(0)
    for i in range(nsub):
        s_cur = s_next
        if i + 1 < nsub:
            s_next = scores(i + 1)
        finish(i, s_cur)


def attention(q, ks, vts):
    B, Hm, Tq, _ = q.shape
    tq = _pick(Tq, (1024, 512, 256, 128))
    nq = Tq // tq
    nseg = len(ks)
    kspec = lambda a: pl.BlockSpec((1, 1) + a.shape[2:], lambda b, h, i: (b, h, 0, 0))
    return pl.pallas_call(
        functools.partial(_attn_kernel, nseg=nseg, nsub=max(1, tq // 256)),
        grid=(B, Hm, nq),
        in_specs=[pl.BlockSpec((1, 1, tq, QK_DIM), lambda b, h, i: (b, h, i, 0))]
                 + [kspec(a) for a in ks] + [kspec(a) for a in vts],
        out_specs=pl.BlockSpec((tq, HEAD_DIM), lambda b, h, i: (b * nq + i, h)),
        out_shape=jax.ShapeDtypeStruct((B * Tq, Hm * HEAD_DIM), BF16),
        compiler_params=_params("parallel", "parallel", "parallel"),
        name=f"attention_{nseg}seg",
    )(q, *ks, *vts)


def _merge_kernel(ac_ref, ag_ref, am_ref, wc_ref, wg_ref, wm_ref, gc_ref, gg_ref, gm_ref, o_ref):
    sig = lambda r: jax.nn.sigmoid(r[...].astype(F32))
    m = (sig(gc_ref) * _dot(ac_ref[...], wc_ref[0]) + sig(gg_ref) * _dot(ag_ref[...], wg_ref[0])
         + sig(gm_ref) * _dot(am_ref[...], wm_ref[0]))
    o_ref[...] = m.astype(o_ref.dtype)


def merge_branches(ac, ag, am, wc, wg, wm, l, u, off):
    N = ac.shape[0]
    D = wc.shape[2]
    tm = _pick(N, (1024, 512, 256, 128))
    tn = _pick((D, off["gate_conv"], off["gate_gdn"], off["gate_mla"]), (512, 256, 128))
    act = lambda a: pl.BlockSpec((tm, a.shape[1]), lambda i, j: (i, 0))
    wsp = lambda w: pl.BlockSpec((1, w.shape[1], tn), lambda i, j: (l, 0, j))
    gate = lambda name: pl.BlockSpec((tm, tn), lambda i, j: (i, off[name] // tn + j))
    return pl.pallas_call(
        _merge_kernel,
        grid=(N // tm, D // tn),
        in_specs=[act(ac), act(ag), act(am), wsp(wc), wsp(wg), wsp(wm),
                  gate("gate_conv"), gate("gate_gdn"), gate("gate_mla")],
        out_specs=pl.BlockSpec((tm, tn), lambda i, j: (i, j)),
        out_shape=jax.ShapeDtypeStruct((N, D), BF16),
        compiler_params=_params("parallel", "parallel"),
        name="merge_branches",
    )(ac, ag, am, wc, wg, wm, u, u, u)


def _out_proj_kernel(m_ref, w_ref, h_ref, g_ref, o_ref):
    o_ref[...] = h_ref[...] + g_ref[0] * _dot(m_ref[...], w_ref[0])


def out_projection(m, w_o, l, h, gate, rows_per_group):
    N, D = h.shape
    tm = _pick(rows_per_group, (1024, 512, 256, 128))
    tn = _pick(D, (512, 256, 128))
    gdiv = rows_per_group // tm
    return pl.pallas_call(
        _out_proj_kernel,
        grid=(N // tm, D // tn),
        in_specs=[pl.BlockSpec((tm, m.shape[1]), lambda i, j: (i, 0)),
                  pl.BlockSpec((1, m.shape[1], tn), lambda i, j: (l, 0, j)),
                  pl.BlockSpec((tm, tn), lambda i, j: (i, j)),
                  pl.BlockSpec((1, 1, tn), lambda i, j: (i // gdiv, 0, j))],
        out_specs=pl.BlockSpec((tm, tn), lambda i, j: (i, j)),
        out_shape=jax.ShapeDtypeStruct((N, D), F32),
        input_output_aliases={2: 0},
        compiler_params=_params("parallel", "parallel"),
        name="out_projection",
    )(m, w_o, h, gate)


def _router_kernel(x_ref, g_ref, sh_ref, sc_ref, r_ref, hn_ref, lg_ref):
    a = _norm_mod(x_ref[...], g_ref[...], sh_ref[0], sc_ref[0]).astype(BF16)
    hn_ref[...] = a
    lg_ref[0] = _dot_nt(r_ref[0], a)


def router(x, g, sh, sc, router_t, l, B, T, rows_per_group):
    N, D = x.shape
    E = router_t.shape[1]
    tm = _pick(T, (1024, 512, 256, 128))
    nt = T // tm
    gdiv = rows_per_group // tm
    return pl.pallas_call(
        _router_kernel,
        grid=(N // tm,),
        in_specs=[pl.BlockSpec((tm, D), lambda i: (i, 0)),
                  pl.BlockSpec((1, D), lambda i: (0, 0)),
                  pl.BlockSpec((1, 1, D), lambda i: (i // gdiv, 0, 0)),
                  pl.BlockSpec((1, 1, D), lambda i: (i // gdiv, 0, 0)),
                  pl.BlockSpec((1, E, D), lambda i: (l, 0, 0))],
        out_specs=[pl.BlockSpec((tm, D), lambda i: (i, 0)),
                   pl.BlockSpec((1, E, tm), lambda i: (i // nt, 0, i % nt))],
        out_shape=[jax.ShapeDtypeStruct((N, D), BF16), jax.ShapeDtypeStruct((B, E, T), F32)],
        compiler_params=_params("parallel"),
        name="router",
    )(x, g.reshape(1, D), sh, sc, router_t)


def _route_kernel(lg_ref, rm_ref, aff_ref, rmt_ref, *, T, E, cap, W):
    lg = lg_ref[0]
    mx = jnp.max(lg, axis=0, keepdims=True)
    ex = jnp.exp(lg - mx)
    aff = ex / jnp.sum(ex, axis=0, keepdims=True)
    aff_ref[0] = aff
    bits = lax.bitcast_convert_type(aff, jnp.int32)

    def count(mask):
        return jnp.sum(jnp.where(mask, 1.0, 0.0), axis=1, keepdims=True)

    thr = jnp.zeros((E, 1), jnp.int32)
    for bit in range(30, -1, -1):
        cand = thr | (1 << bit)
        thr = jnp.where(count(bits >= cand) >= cap, cand, thr)
    gt = bits > thr
    eq = bits == thr
    need = cap - count(gt)

    ri = lax.broadcasted_iota(jnp.int32, (W, W), 0)
    ci = lax.broadcasted_iota(jnp.int32, (W, W), 1)
    upper = jnp.where(ri < ci, 1.0, 0.0).astype(BF16)
    ident = jnp.where(ri == ci, 1.0, 0.0).astype(BF16)

    def excl_cumsum(mask):
        m = jnp.where(mask, 1.0, 0.0)
        outs, carry = [], jnp.zeros((E, 1), F32)
        for j in range(T // W):
            blk = m[:, j * W:(j + 1) * W]
            outs.append(_dot(blk.astype(BF16), upper) + carry)
            carry = carry + jnp.sum(blk, axis=1, keepdims=True)
        return jnp.concatenate(outs, axis=1) if len(outs) > 1 else outs[0]

    sel = gt | (eq & (excl_cumsum(eq) < need))
    rm = jnp.where(sel, excl_cumsum(sel), -1.0)
    rm_ref[0] = rm
    rmb = rm.astype(BF16)
    for j in range(T // W):
        rmt_ref[0, j * W:(j + 1) * W, :] = _dot_nt(ident, rmb[:, j * W:(j + 1) * W])


def route(logits_t, cap):
    B, E, T = logits_t.shape
    W = 256 if T % 256 == 0 else 128
    blk = pl.BlockSpec((1, E, T), lambda b: (b, 0, 0))
    return pl.pallas_call(
        functools.partial(_route_kernel, T=T, E=E, cap=cap, W=W),
        grid=(B,),
        in_specs=[blk],
        out_specs=[blk, blk, pl.BlockSpec((1, T, E), lambda b: (b, 0, 0))],
        out_shape=[jax.ShapeDtypeStruct((B, E, T), F32), jax.ShapeDtypeStruct((B, E, T), F32),
                   jax.ShapeDtypeStruct((B, T, E), F32)],
        compiler_params=_params("parallel"),
        name="route",
    )(logits_t)


def _gather_kernel(rm_ref, aff_ref, hn_ref, xs_ref, gate_ref, *, cap):
    rm = rm_ref[0, 0]
    T = rm.shape[1]
    slot = lax.broadcasted_iota(jnp.int32, (cap, T), 0).astype(F32)
    hit = slot == rm
    xs_ref[0, 0] = _dot(jnp.where(hit, 1.0, 0.0).astype(BF16), hn_ref[...]).astype(xs_ref.dtype)
    gate_ref[0, 0] = jnp.sum(jnp.where(hit, aff_ref[0, 0], 0.0), axis=1, keepdims=True)


def moe_gather(rm, aff, hn, cap):
    B, E, T = rm.shape
    D = hn.shape[1]
    rowspec = pl.BlockSpec((1, 1, 1, T), lambda b, e: (b, e, 0, 0))
    return pl.pallas_call(
        functools.partial(_gather_kernel, cap=cap),
        grid=(B, E),
        in_specs=[rowspec, rowspec, pl.BlockSpec((T, D), lambda b, e: (b, 0))],
        out_specs=[pl.BlockSpec((1, 1, cap, D), lambda b, e: (b, e, 0, 0)),
                   pl.BlockSpec((1, 1, cap, 1), lambda b, e: (b, e, 0, 0))],
        out_shape=[jax.ShapeDtypeStruct((B, E, cap, D), BF16), jax.ShapeDtypeStruct((B, E, cap, 1), F32)],
        compiler_params=_params("parallel", "parallel"),
        name="moe_gather",
    )(rm.reshape(B, E, 1, T), aff.reshape(B, E, 1, T), hn)


def _ffn_kernel(xs_ref, gate_ref, wg_ref, wu_ref, wd_ref, ye_ref):
    bb, _, cap, D = xs_ref.shape
    xs = xs_ref[...].reshape(bb * cap, D)
    hid = (_silu(_dot(xs, wg_ref[0, 0])) * _dot(xs, wu_ref[0, 0])).astype(BF16)
    ye = _dot(hid, wd_ref[0, 0]) * gate_ref[...].reshape(bb * cap, 1)
    ye_ref[...] = ye.reshape(bb, 1, cap, D).astype(ye_ref.dtype)


def moe_ffn(xs, gate, w_gate, w_up, w_down, l):
    B, E, cap, D = xs.shape
    FF = w_gate.shape[3]
    bb = max(1, min(B, 512 // cap))
    while B % bb:
        bb -= 1
    tok = lambda last: pl.BlockSpec((bb, 1, cap, last), lambda e, b: (b, e, 0, 0))
    return pl.pallas_call(
        _ffn_kernel,
        grid=(E, B // bb),
        in_specs=[tok(D), tok(1),
                  pl.BlockSpec((1, 1, D, FF), lambda e, b: (l, e, 0, 0)),
                  pl.BlockSpec((1, 1, D, FF), lambda e, b: (l, e, 0, 0)),
                  pl.BlockSpec((1, 1, FF, D), lambda e, b: (l, e, 0, 0))],
        out_specs=tok(D),
        out_shape=jax.ShapeDtypeStruct((B, E, cap, D), BF16),
        compiler_params=_params("parallel", "parallel"),
        name="moe_ffn",
    )(xs, gate, w_gate, w_up, w_down)


def _scatter_kernel(rmt_ref, ye_ref, h_ref, g_ref, o_ref, *, E, cap):
    tt = rmt_ref.shape[1]
    rmt = rmt_ref[0]
    slot = lax.broadcasted_iota(jnp.int32, (tt, cap), 1).astype(F32)
    acc = jnp.zeros(o_ref.shape, F32)
    for e in range(E):
        onehot = jnp.where(rmt[:, e:e + 1] == slot, 1.0, 0.0).astype(BF16)
        acc = acc + _dot(onehot, ye_ref[0, e * cap:(e + 1) * cap, :])
    o_ref[...] = h_ref[...] + g_ref[0] * acc


def moe_scatter(rmt, ye, h, gate, T):
    B, _, E = rmt.shape
    _, EC, D = ye.shape
    cap = EC // E
    tt = _pick(T, (512, 256, 128))
    tn = _pick(D, (1024, 512, 256, 128))
    nt = T // tt
    return pl.pallas_call(
        functools.partial(_scatter_kernel, E=E, cap=cap),
        grid=(B, D // tn, nt),
        in_specs=[pl.BlockSpec((1, tt, E), lambda b, j, i: (b, i, 0)),
                  pl.BlockSpec((1, EC, tn), lambda b, j, i: (b, 0, j)),
                  pl.BlockSpec((tt, tn), lambda b, j, i: (b * nt + i, j)),
                  pl.BlockSpec((1, 1, tn), lambda b, j, i: (b, 0, j))],
        out_specs=pl.BlockSpec((tt, tn), lambda b, j, i: (b * nt + i, j)),
        out_shape=jax.ShapeDtypeStruct(h.shape, F32),
        input_output_aliases={2: 0},
        compiler_params=_params("parallel", "parallel", "parallel"),
        name="moe_scatter",
    )(rmt, ye, h, gate)


def expert_choice_moe(h, g, sh, sc, gate, router_t, w_gate, w_up, w_down, l, B, T, rows_per_group):
    E = router_t.shape[1]
    cap = EC_CAPACITY * T // E
    assert cap <= 256 and cap % 16 == 0, "slot indices must stay exact in bf16 and fill bf16 sublane tiles"
    hn, logits_t = router(h, g, sh, sc, router_t, l, B, T, rows_per_group)
    rm, aff, rmt = route(logits_t, cap)
    xs, gsl = moe_gather(rm, aff, hn, cap)
    ye = moe_ffn(xs, gsl, w_gate, w_up, w_down, l)
    return moe_scatter(rmt, ye.reshape(B, E * cap, ye.shape[-1]), h, gate, T)


def _final_norm_kernel(x_ref, g_ref, o_ref):
    o_ref[...] = _rms(x_ref[...], g_ref[...])


def final_norm(x, g):
    N, D = x.shape
    tm = _pick(N, (512, 256, 128))
    return pl.pallas_call(
        _final_norm_kernel,
        grid=(N // tm,),
        in_specs=[pl.BlockSpec((tm, D), lambda i: (i, 0)), pl.BlockSpec((1, D), lambda i: (0, 0))],
        out_specs=pl.BlockSpec((tm, D), lambda i: (i, 0)),
        out_shape=jax.ShapeDtypeStruct((N, D), F32),
        compiler_params=_params("parallel"),
        name="final_norm",
    )(x, g.reshape(1, D))


def _swap_rope(w):
    half = ROPE_DIM // 2
    return jnp.concatenate([-w[..., half:], w[..., :half]], axis=-1)


_PACKED_ORDER = ("mla_cq", "gdn_small", "mla_kpe", "mla_ckv", "conv_x", "conv_b", "conv_c", "gdn_q", "gdn_k", "gdn_v",
                 "gdn_z", "gate_conv", "gate_gdn", "gate_mla")


def _in_layout(C, Hg, RQ, RKV, D):
    width = {"mla_cq": RQ, "gdn_small": LANE, "mla_kpe": LANE, "mla_ckv": RKV, "conv_x": C, "conv_b": C, "conv_c": C,
             "gdn_q": Hg * HEAD_DIM, "gdn_k": Hg * HEAD_DIM, "gdn_v": Hg * HEAD_DIM, "gdn_z": Hg * HEAD_DIM,
             "gate_conv": D, "gate_gdn": D, "gate_mla": D}
    off, o = {}, 0
    for name in _PACKED_ORDER:
        off[name] = o
        o += width[name]
    assert off["mla_cq"] % RQ == 0 and off["mla_ckv"] % RKV == 0
    return off, o


def _pack_w_in(w, C, Hg, RQ, RKV, D):
    names = ("conv_x", "conv_b", "conv_c", "gdn_q", "gdn_k", "gdn_v", "gdn_z", "gdn_small", "mla_cq", "mla_ckv",
             "mla_kpe", "gate_conv", "gate_gdn", "gate_mla")
    sizes = (C, C, C, Hg * HEAD_DIM, Hg * HEAD_DIM, Hg * HEAD_DIM, Hg * HEAD_DIM, 4 * Hg, RQ, RKV, ROPE_DIM, D, D, D)
    seg, o = {}, 0
    for name, size in zip(names, sizes):
        seg[name] = w[..., o:o + size]
        o += size
    seg["gdn_small"] = jnp.pad(seg["gdn_small"], ((0, 0), (0, 0), (0, LANE - 4 * Hg)))
    seg["mla_kpe"] = jnp.concatenate([seg["mla_kpe"], _swap_rope(seg["mla_kpe"])], axis=-1)
    return jnp.concatenate([seg[n] for n in _PACKED_ORDER], axis=-1).astype(BF16)


def _rope_tables(T):
    rows = T // GRID_W
    row = jnp.repeat(jnp.arange(rows), GRID_W)
    col = jnp.tile(jnp.arange(GRID_W), rows)
    n_freq = ROPE_DIM // 4
    inv_freq = 1.0 / (ROPE_THETA ** (jnp.arange(n_freq, dtype=F32) / n_freq))
    ang = jnp.concatenate([row[:, None] * inv_freq, col[:, None] * inv_freq], axis=-1)
    cos, sin = jnp.cos(ang), jnp.sin(ang)
    return jnp.concatenate([cos, cos], axis=1), jnp.concatenate([sin, sin], axis=1)


def kernel(x, c, ctx, c_ctx, w_ada, b_ada, norm1_g, w_in, conv_w, conv_out, gdn_conv_w, gdn_a_log, gdn_dt_bias,
           gdn_norm_g, gdn_out, mla_q_norm_g, mla_w_uq, mla_kv_norm_g, mla_w_ukv, mla_out, w_o, norm2_g, router_w,
           w_gate, w_up, w_down, final_g):
    B, T, D = x.shape
    TC = ctx.shape[1]
    L = w_ada.shape[0]
    C = conv_w.shape[-1]
    Hg = gdn_a_log.shape[-1]
    RQ = mla_q_norm_g.shape[-1]
    RKV = mla_kv_norm_g.shape[-1]
    Hm = mla_w_uq.shape[-1] // QK_DIM
    assert 4 * Hg <= LANE and T % GDN_CHUNK == 0 and TC % GDN_CHUNK == 0

    off, _ = _in_layout(C, Hg, RQ, RKV, D)

    R = -(-(B + 1) // 8) * 8
    cc = jnp.concatenate([c, c_ctx[None, :], jnp.zeros((R - B - 1, D), F32)], axis=0)
    mod = ada_modulation(cc, w_ada, b_ada)

    cos64, sin64 = _rope_tables(T)
    q_ct, q_st = jnp.tile(cos64, (1, Hm)), jnp.tile(sin64, (1, Hm))
    q_ct0, q_st0 = jnp.ones((TC, Hm * ROPE_DIM), F32), jnp.zeros((TC, Hm * ROPE_DIM), F32)
    k_tab = jnp.concatenate([cos64, sin64], 1)
    k_tab0 = jnp.concatenate([jnp.ones((TC, ROPE_DIM), F32), jnp.zeros((TC, ROPE_DIM), F32)], 1)

    h = x.reshape(B * T, D)
    hc = ctx.reshape(B * TC, D)
    zero_state = jnp.zeros((B, Hg, HEAD_DIM, HEAD_DIM), F32)

    w_in_p = _pack_w_in(w_in, C, Hg, RQ, RKV, D)
    wc_o, wg_o, wm_o, wo_b = (a.astype(BF16) for a in (conv_out, gdn_out, mla_out, w_o))
    router_t = jnp.swapaxes(router_w, 1, 2).astype(BF16)
    wgt, wup, wdn = w_gate.astype(BF16), w_up.astype(BF16), w_down.astype(BF16)

    for l in range(L):
        need_ctx = l < L - 1
        wq = mla_w_uq[l].reshape(RQ, Hm, QK_DIM)
        wq_pe = wq[..., HEAD_DIM:]
        wq_main = jnp.concatenate([wq[..., :HEAD_DIM].reshape(RQ, -1), wq_pe.reshape(RQ, -1)], 1).astype(BF16)
        wq_aux = _swap_rope(wq_pe).reshape(RQ, -1).astype(BF16)
        wkv = mla_w_ukv[l].reshape(RKV, Hm, 2 * HEAD_DIM)
        w_k = wkv[..., :HEAD_DIM].reshape(RKV, -1).astype(BF16)
        w_vt = wkv[..., HEAD_DIM:].transpose(1, 2, 0).astype(BF16)
        pad = jnp.zeros((LANE - 2 * Hg,), F32)
        alog_row = jnp.concatenate([gdn_a_log[l].reshape(-1), pad]).reshape(1, LANE)
        dtb_row = jnp.concatenate([gdn_dt_bias[l].reshape(-1), pad]).reshape(1, LANE)

        ml = mod[l, :B].reshape(B, 1, 6 * D)
        mc = mod[l, B:B + 1].reshape(1, 1, 6 * D)
        sh1, sc1, g1, sh2, sc2, g2 = (ml[..., i * D:(i + 1) * D] for i in range(6))
        csh1, csc1, cg1, csh2, csc2, cg2 = (mc[..., i * D:(i + 1) * D] for i in range(6))

        u_lat = in_projection(h, norm1_g[l], sh1, sc1, w_in_p, l, T)
        u_ctx = in_projection(hc, norm1_g[l], csh1, csc1, w_in_p, l, B * TC)

        ag_ctx, s_f, s_b = gdn_mixer(u_ctx, gdn_conv_w[l], alog_row, dtb_row, gdn_norm_g[l], zero_state, zero_state,
                                     B, TC, Hg, off)
        ag_lat, _, _ = gdn_mixer(u_lat, gdn_conv_w[l], alog_row, dtb_row, gdn_norm_g[l], s_f, s_b, B, T, Hg, off)

        k_ctx, v_ctx = mla_keys_values(u_ctx, mla_kv_norm_g[l], w_k, w_vt, k_tab0, B, TC, Hm, off)
        k_lat, v_lat = mla_keys_values(u_lat, mla_kv_norm_g[l], w_k, w_vt, k_tab, B, T, Hm, off)
        q_lat = mla_queries(u_lat, mla_q_norm_g[l], wq_main, wq_aux, q_ct, q_st, B, T, Hm, off)
        am_lat = attention(q_lat, [k_ctx, k_lat], [v_ctx, v_lat])

        ac_lat = conv_mixer_front(u_lat, conv_w[l], B, T, off)
        m_lat = merge_branches(ac_lat, ag_lat, am_lat, wc_o, wg_o, wm_o, l, u_lat, off)
        h = out_projection(m_lat, wo_b, l, h, g1, T)
        h = expert_choice_moe(h, norm2_g[l], sh2, sc2, g2, router_t, wgt, wup, wdn, l, B, T, T)

        if need_ctx:
            q_ctx = mla_queries(u_ctx, mla_q_norm_g[l], wq_main, wq_aux, q_ct0, q_st0, B, TC, Hm, off)
            am_ctx = attention(q_ctx, [k_ctx], [v_ctx])
            ac_ctx = conv_mixer_front(u_ctx, conv_w[l], B, TC, off)
            m_ctx = merge_branches(ac_ctx, ag_ctx, am_ctx, wc_o, wg_o, wm_o, l, u_ctx, off)
            hc = out_projection(m_ctx, wo_b, l, hc, cg1, B * TC)
            cg2b = jnp.broadcast_to(cg2, (B, 1, D))
            hc = expert_choice_moe(hc, norm2_g[l], csh2, csc2, cg2b, router_t, wgt, wup, wdn, l, B, TC, B * TC)

    return final_norm(h, final_g).reshape(B, T, D)
```

```python
import functools
import math

import jax
import jax.numpy as jnp
from jax import lax
from jax.experimental import pallas as pl
from jax.experimental.pallas import tpu as pltpu

F32 = jnp.float32
BF16 = jnp.bfloat16

RMS_EPS = 1e-6
L2_EPS = 1e-6
ROPE_THETA = 10000.0
GRID_W = 64
HEAD_DIM = 128
ROPE_DIM = 64
QK_DIM = HEAD_DIM + ROPE_DIM
GDN_CHUNK = 128
GDN_SUB = 16
EC_CAPACITY = 2
LANE = 128
V_PAD = 16
ATTN_LOGIT_SCALE = QK_DIM ** -0.5 * math.log2(math.e)
VMEM_LIMIT = 56 * 1024 * 1024


def _pick(n, cands):
    ns = n if isinstance(n, tuple) else (n,)
    for c in cands:
        if all(v % c == 0 for v in ns):
            return c
    raise ValueError(f"no tile in {cands} divides {ns}")


def _params(*sem):
    return pltpu.CompilerParams(dimension_semantics=sem, vmem_limit_bytes=VMEM_LIMIT)


def _dot(a, b):
    return jnp.dot(a, b, preferred_element_type=F32)


def _dot_nt(a, b):
    return lax.dot_general(a, b, (((1,), (1,)), ((), ())), preferred_element_type=F32)


def _dot_tn(a, b):
    return lax.dot_general(a, b, (((0,), (0,)), ((), ())), preferred_element_type=F32)


def _split2(x):
    hi = x.astype(BF16)
    lo = (x - hi.astype(F32)).astype(BF16)
    return hi, lo


def _split3(x):
    hi = x.astype(BF16)
    r = x - hi.astype(F32)
    mid = r.astype(BF16)
    lo = (r - mid.astype(F32)).astype(BF16)
    return hi, mid, lo


def _silu(x):
    return x * jax.nn.sigmoid(x)


def _softplus(x):
    return jnp.maximum(x, 0.0) + jnp.log(1.0 + jnp.exp(-jnp.abs(x)))


def _ada_kernel(c_ref, w_ref, b_ref, o_ref):
    a = _silu(c_ref[...])
    ah, al = _split2(a)
    wh, wl = _split2(w_ref[0])
    o_ref[0] = _dot(ah, wh) + (_dot(ah, wl) + _dot(al, wh)) + b_ref[0]


def ada_modulation(cc, w_ada, b_ada):
    L, D, D6 = w_ada.shape
    R = cc.shape[0]
    tn = _pick(D6, (1024, 512, 256, 128))
    return pl.pallas_call(
        _ada_kernel,
        grid=(L, D6 // tn),
        in_specs=[pl.BlockSpec((R, D), lambda l, j: (0, 0)),
                  pl.BlockSpec((1, D, tn), lambda l, j: (l, 0, j)),
                  pl.BlockSpec((1, 1, tn), lambda l, j: (l, 0, j))],
        out_specs=pl.BlockSpec((1, R, tn), lambda l, j: (l, 0, j)),
        out_shape=jax.ShapeDtypeStruct((L, R, D6), F32),
        compiler_params=_params("parallel", "parallel"),
        name="ada_modulation",
    )(cc, w_ada, b_ada.reshape(L, 1, D6))


def _norm_mod(x, g, sh, sc):
    ms = jnp.mean(x * x, axis=-1, keepdims=True)
    y = x * lax.rsqrt(ms + RMS_EPS) * g
    return y * (1.0 + sc) + sh


def _in_proj_kernel(x_ref, g_ref, sh_ref, sc_ref, w_ref, o_ref, a_scr):
    @pl.when(pl.program_id(1) == 0)
    def _():
        a_scr[...] = _norm_mod(x_ref[...], g_ref[...], sh_ref[0], sc_ref[0]).astype(BF16)

    o_ref[...] = _dot(a_scr[...], w_ref[0]).astype(o_ref.dtype)


def in_projection(x, g, sh, sc, w, l, rows_per_group):
    N, D = x.shape
    NC = w.shape[2]
    tm = _pick(rows_per_group, (1024, 512, 256, 128))
    tn = _pick(NC, (512, 256, 128))
    gdiv = rows_per_group // tm
    return pl.pallas_call(
        _in_proj_kernel,
        grid=(N // tm, NC // tn),
        in_specs=[pl.BlockSpec((tm, D), lambda i, j: (i, 0)),
                  pl.BlockSpec((1, D), lambda i, j: (0, 0)),
                  pl.BlockSpec((1, 1, D), lambda i, j: (i // gdiv, 0, 0)),
                  pl.BlockSpec((1, 1, D), lambda i, j: (i // gdiv, 0, 0)),
                  pl.BlockSpec((1, D, tn), lambda i, j: (l, 0, j))],
        out_specs=pl.BlockSpec((tm, tn), lambda i, j: (i, j)),
        out_shape=jax.ShapeDtypeStruct((N, NC), BF16),
        scratch_shapes=[pltpu.VMEM((tm, D), BF16)],
        compiler_params=_params("parallel", "arbitrary"),
        name="in_projection",
    )(x, g.reshape(1, D), sh, sc, w)


def _conv3(x, w, T):
    row = lax.broadcasted_iota(jnp.int32, x.shape, 0)
    prev = jnp.where(row == 0, 0.0, pltpu.roll(x, 1, 0))
    nxt = jnp.where(row == T - 1, 0.0, pltpu.roll(x, T - 1, 0))
    return prev * w[0:1] + x * w[1:2] + nxt * w[2:3]


def _conv_mixer_kernel(x_ref, b_ref, c_ref, w_ref, o_ref, *, T):
    v = c_ref[...].astype(F32) * x_ref[...].astype(F32)
    o_ref[...] = (b_ref[...].astype(F32) * _conv3(v, w_ref[...], T)).astype(o_ref.dtype)


def conv_mixer_front(u, conv_w, B, T, off):
    C = conv_w.shape[1]
    tc = _pick(C, (256, 128))
    ox, ob, oc = (off[k] // tc for k in ("conv_x", "conv_b", "conv_c"))
    return pl.pallas_call(
        functools.partial(_conv_mixer_kernel, T=T),
        grid=(B, C // tc),
        in_specs=[pl.BlockSpec((T, tc), lambda b, j: (b, ox + j)),
                  pl.BlockSpec((T, tc), lambda b, j: (b, ob + j)),
                  pl.BlockSpec((T, tc), lambda b, j: (b, oc + j)),
                  pl.BlockSpec((3, tc), lambda b, j: (0, j))],
        out_specs=pl.BlockSpec((T, tc), lambda b, j: (b, j)),
        out_shape=jax.ShapeDtypeStruct((B * T, C), BF16),
        compiler_params=_params("parallel", "parallel"),
        name="conv_mixer_front",
    )(u, u, u, conv_w)


def _gdn_solve(a_list, rhs_list, sub_mask, eye):
    g = a_list[0].shape[0]
    b16 = lambda ts: [t.astype(BF16) for t in ts]
    dots = lambda xs, ys: [_dot(x, y) for x, y in zip(xs, ys)]
    d = [jnp.where(sub_mask, a, 0.0) for a in a_list]
    n = [a - dd for a, dd in zip(a_list, d)]
    db = b16(d)
    d2b = b16(dots(db, db))
    d4b = b16(dots(d2b, d2b))
    d8b = b16(dots(d4b, d4b))
    x = [eye - dd for dd in d]
    for p in (d2b, d4b, d8b):
        x = [xx + t for xx, t in zip(x, dots(b16(x), p))]
    z = dots(b16(x), b16([jnp.concatenate([nn, r], axis=1) for nn, r in zip(n, rhs_list)]))
    mb = b16([zz[:, :g] for zz in z])
    zr = [zz[:, g:] for zz in z]
    powers = [mb]
    for _ in range(int(math.log2(GDN_CHUNK // GDN_SUB)) - 1):
        powers.append(b16(dots(powers[-1], powers[-1])))
    for p in reversed(powers[1:]):
        zr = [r + t for r, t in zip(zr, dots(p, b16(zr)))]
    return [r - t for r, t in zip(zr, dots(mb, b16(zr)))]


def _gdn_kernel(q_ref, k_ref, v_ref, z_ref, ab_ref, wq_ref, wk_ref, wv_ref, alog_ref, dtb_ref, ng_ref,
                s0f_ref, s0b_ref, o_ref, sf_ref, sb_ref,
                q_s, k_s, v_s, col_s, u_s, w_s, qg_s, kd_s, in_s, eg_s, o_s, st_s, *, T, H, G, GI, HB):
    hb = pl.program_id(1)
    NC = T // GDN_CHUNK
    L = GDN_CHUNK

    def l2n(x):
        return x * lax.rsqrt(jnp.sum(x * x, axis=-1, keepdims=True) + L2_EPS)

    ri = lax.broadcasted_iota(jnp.int32, (G, G), 0)
    ci = lax.broadcasted_iota(jnp.int32, (G, G), 1)
    same_chunk = (ri >> int(math.log2(L))) == (ci >> int(math.log2(L)))
    same_sub = (ri >> int(math.log2(GDN_SUB))) == (ci >> int(math.log2(GDN_SUB)))
    eye = jnp.where(ri == ci, 1.0, 0.0).astype(F32)
    ones_chunk = jnp.where(same_chunk, 1.0, 0.0).astype(BF16)

    ab = ab_ref[...].astype(F32)
    lane = lax.broadcasted_iota(jnp.int32, ab.shape, 1)
    gfull = -jnp.exp(alog_ref[...]) * _softplus(ab + dtb_ref[...])
    bfull = jax.nn.sigmoid(ab)

    def col(x, idx):
        return jnp.sum(jnp.where(lane == idx, x, 0.0), axis=1, keepdims=True)

    for hh in range(HB):
        hsl = slice(hh * HEAD_DIM, (hh + 1) * HEAD_DIM)
        head = hb * HB + hh

        def prep(x_ref, w_ref):
            return _silu(_conv3(x_ref[:, hsl].astype(F32), w_ref[:, hsl], T))

        q_s[hh] = l2n(prep(q_ref, wq_ref)) * (HEAD_DIM ** -0.5)
        k_s[hh] = l2n(prep(k_ref, wk_ref))
        v_s[hh] = prep(v_ref, wv_ref)

        g_f, g_b = col(gfull, head), col(gfull, H + head)
        b_f, b_b = col(bfull, 2 * H + head), col(bfull, 3 * H + head)
        col_s[hh] = jnp.where(lane == 0, g_f, jnp.where(lane == 1, g_b, jnp.where(lane == 2, b_f, b_b)))
        st_s[hh, 0] = s0f_ref[0, hh]
        st_s[hh, 1] = s0b_ref[0, hh]

    chains = [(hh, d) for hh in range(HB) for d in range(2)]
    incl = [same_chunk & (ci <= ri), same_chunk & (ci >= ri)]
    strict = [same_chunk & (ci < ri), same_chunk & (ci > ri)]
    incl_b = [jnp.where(m, 1.0, 0.0).astype(BF16) for m in incl]
    ones_8g = jnp.ones((8, G), BF16)

    def exact3_list(ms, xs):
        parts = [_split3(x) for x in xs]
        return [_dot(m, p[0]) + _dot(m, p[1]) + _dot(m, p[2]) for m, p in zip(ms, parts)]

    def group_body(gi, carry):
        units = [(j, hh) for j in range(GI) for hh in range(HB)]
        gch = [(ui, d) for ui in range(len(units)) for d in range(2)]
        r0 = [pl.multiple_of((gi * GI + j) * G, G) for j in range(GI)]
        rows = [pl.ds(r0[j], G) for j, _ in units]
        qg = [q_s[hh, r, :] for (_, hh), r in zip(units, rows)]
        kg = [k_s[hh, r, :] for (_, hh), r in zip(units, rows)]
        vg = [v_s[hh, r, :] for (_, hh), r in zip(units, rows)]
        cols = [col_s[hh, r, :] for (_, hh), r in zip(units, rows)]
        kb16 = [k.astype(BF16) for k in kg]
        kk = [_dot_nt(kb, kb) for kb in kb16]
        qk = [_dot_nt(q.astype(BF16), kb) for q, kb in zip(qg, kb16)]
        bcol = [cols[ui][:, 2 + d:3 + d] for ui, d in gch]
        gb = [jnp.broadcast_to(cols[ui][:, d:d + 1], (G, LANE)) for ui, d in gch]
        gc = exact3_list([incl_b[d] for _, d in gch], gb)
        gl = exact3_list([ones_chunk] * len(gch), gb)
        cmat = [jnp.concatenate([g] * (G // LANE), axis=1) if G > LANE else g for g in gc]
        rrow = [r[0:1] for r in exact3_list([ones_8g] * len(gch),
                                            [jnp.where(ri == ci, cm, 0.0) for cm in cmat])]
        decay = [jnp.where(incl[d], jnp.exp(jnp.where(incl[d], cm - rr, 0.0)), 0.0)
                 for (_, d), cm, rr in zip(gch, cmat, rrow)]
        a = [jnp.where(strict[d], bc * kk[ui] * dc, 0.0) for (ui, d), bc, dc in zip(gch, bcol, decay)]
        egc = [jnp.exp(g) for g in gc]
        rhs = [jnp.concatenate([vg[ui] * bc, kg[ui] * bc * e], axis=1) for (ui, _), bc, e in zip(gch, bcol, egc)]
        uw = _gdn_solve(a, rhs, same_sub, eye)
        for i, (ui, d) in enumerate(gch):
            j, hh = units[ui]
            u_s[hh, d, rows[ui], :] = uw[i][:, :HEAD_DIM]
            w_s[hh, d, rows[ui], :] = uw[i][:, HEAD_DIM:].astype(BF16)
            qg_s[hh, d, rows[ui], :] = (qg[ui] * egc[i]).astype(BF16)
            kd_s[hh, d, rows[ui], :] = (kg[ui] * jnp.exp(gl[i] - gc[i])).astype(BF16)
            egl = jnp.exp(gl[i])
            intra = (qk[ui] * decay[i]).astype(BF16)
            for c in range(G // L):
                e0 = pl.multiple_of(((gi * GI + j) * (G // L) + c) * 8, 8)
                eg_s[hh, d, pl.ds(e0, 8), :] = egl[c * L:c * L + 8]
                in_s[hh, d, pl.ds(r0[j] + c * L, L), :] = intra[c * L:(c + 1) * L, c * L:(c + 1) * L]
        return carry

    lax.fori_loop(0, T // (G * GI), group_body, 0)

    def scan_body(c, carry):
        rws = [pl.ds(pl.multiple_of((c if d == 0 else NC - 1 - c) * L, L), L) for _, d in chains]
        egr = [pl.ds(pl.multiple_of((c if d == 0 else NC - 1 - c) * 8, 8), 8) for _, d in chains]
        s = [st_s[hh, d] for hh, d in chains]
        wq = [jnp.concatenate([w_s[hh, d, r, :], qg_s[hh, d, r, :]], axis=0) for (hh, d), r in zip(chains, rws)]
        ws = [_dot(x, y.astype(BF16)) for x, y in zip(wq, s)]
        vb = [(u_s[hh, d, r, :] - w[:L]).astype(BF16) for (hh, d), r, w in zip(chains, rws, ws)]
        oi = [_dot(in_s[hh, d, r, :], v) for (hh, d), r, v in zip(chains, rws, vb)]
        ds_ = [_dot_tn(kd_s[hh, d, r, :], v) for (hh, d), r, v in zip(chains, rws, vb)]
        for i, (hh, d) in enumerate(chains):
            o_s[hh, d, rws[i], :] = ws[i][L:] + oi[i]
            st_s[hh, d] = s[i] * eg_s[hh, d, egr[i], :][0:1] + ds_[i]
        return carry

    lax.fori_loop(0, NC, scan_body, 0)

    for hh in range(HB):
        hsl = slice(hh * HEAD_DIM, (hh + 1) * HEAD_DIM)
        sf_ref[0, hh] = st_s[hh, 0]
        sb_ref[0, hh] = st_s[hh, 1]
        o = o_s[hh, 0] + o_s[hh, 1]
        y = o * lax.rsqrt(jnp.mean(o * o, axis=-1, keepdims=True) + RMS_EPS) * ng_ref[...]
        o_ref[:, hsl] = (y * _silu(z_ref[:, hsl].astype(F32))).astype(o_ref.dtype)


def gdn_mixer(u, conv_w, alog_row, dtb_row, norm_g, s0f, s0b, B, T, H, off):
    G = GDN_CHUNK
    assert G == LANE
    GI = 2 if T % (2 * G) == 0 else 1
    HB = 2 if H % 2 == 0 else 1
    W = HB * HEAD_DIM
    assert all(off[k] % W == 0 for k in ("gdn_q", "gdn_k", "gdn_v", "gdn_z"))
    oq, ok, ov, oz = (off[k] // W for k in ("gdn_q", "gdn_k", "gdn_v", "gdn_z"))
    oab = off["gdn_small"] // LANE
    seq = lambda o: pl.BlockSpec((T, W), lambda b, h: (b, o + h))
    cw = lambda o: pl.BlockSpec((3, W), lambda b, h: (0, o * (H // HB) + h))
    row = pl.BlockSpec((1, LANE), lambda b, h: (0, 0))
    st = pl.BlockSpec((1, HB, HEAD_DIM, HEAD_DIM), lambda b, h: (b, h, 0, 0))
    tbuf = lambda: pltpu.VMEM((HB, T, HEAD_DIM), F32)
    hbuf = lambda dt: pltpu.VMEM((HB, 2, T, HEAD_DIM), dt)
    return pl.pallas_call(
        functools.partial(_gdn_kernel, T=T, H=H, G=G, GI=GI, HB=HB),
        grid=(B, H // HB),
        in_specs=[seq(oq), seq(ok), seq(ov), seq(oz),
                  pl.BlockSpec((T, LANE), lambda b, h: (b, oab)),
                  cw(0), cw(1), cw(2), row, row, row, st, st],
        out_specs=[pl.BlockSpec((T, W), lambda b, h: (b, h)), st, st],
        out_shape=[jax.ShapeDtypeStruct((B * T, H * HEAD_DIM), BF16),
                   jax.ShapeDtypeStruct((B, H, HEAD_DIM, HEAD_DIM), F32),
                   jax.ShapeDtypeStruct((B, H, HEAD_DIM, HEAD_DIM), F32)],
        scratch_shapes=[tbuf(), tbuf(), tbuf(), tbuf(), hbuf(F32), hbuf(BF16), hbuf(BF16), hbuf(BF16),
                        pltpu.VMEM((HB, 2, T, GDN_CHUNK), BF16), pltpu.VMEM((HB, 2, T // 8, HEAD_DIM), F32),
                        hbuf(F32), pltpu.VMEM((HB, 2, HEAD_DIM, HEAD_DIM), F32)],
        compiler_params=_params("parallel", "parallel"),
        name="gdn_mixer",
    )(u, u, u, u, u, conv_w, conv_w, conv_w, alog_row, dtb_row, norm_g.reshape(1, HEAD_DIM), s0f, s0b)


def _rms(x, g):
    return x * lax.rsqrt(jnp.mean(x * x, axis=-1, keepdims=True) + RMS_EPS) * g


def _mla_q_kernel(x_ref, g_ref, wm_ref, wa_ref, ct_ref, st_ref, o_ref, *, Hm):
    a = _rms(x_ref[...].astype(F32), g_ref[...]).astype(BF16)
    main = _dot(a, wm_ref[...]) * ATTN_LOGIT_SCALE
    nn = Hm * HEAD_DIM
    pe = main[:, nn:] * ct_ref[...] + (_dot(a, wa_ref[...]) * ATTN_LOGIT_SCALE) * st_ref[...]
    for h in range(Hm):
        o_ref[0, h, :, 0:HEAD_DIM] = main[:, h * HEAD_DIM:(h + 1) * HEAD_DIM].astype(o_ref.dtype)
        o_ref[0, h, :, HEAD_DIM:QK_DIM] = pe[:, h * ROPE_DIM:(h + 1) * ROPE_DIM].astype(o_ref.dtype)


def mla_queries(u, norm_g, w_main, w_aux, ctab, stab, B, T, Hm, off):
    R = w_main.shape[0]
    tm = _pick(T, (512, 256, 128))
    nt = T // tm
    oc = off["mla_cq"] // R
    full = lambda a: pl.BlockSpec(a.shape, lambda i: (0, 0))
    tab = pl.BlockSpec((tm, Hm * ROPE_DIM), lambda i: (i % nt, 0))
    return pl.pallas_call(
        functools.partial(_mla_q_kernel, Hm=Hm),
        grid=(B * nt,),
        in_specs=[pl.BlockSpec((tm, R), lambda i: (i, oc)), pl.BlockSpec((1, R), lambda i: (0, 0)),
                  full(w_main), full(w_aux), tab, tab],
        out_specs=pl.BlockSpec((1, Hm, tm, QK_DIM), lambda i: (i // nt, 0, i % nt, 0)),
        out_shape=jax.ShapeDtypeStruct((B, Hm, T, QK_DIM), BF16),
        compiler_params=_params("parallel"),
        name="mla_queries",
    )(u, norm_g.reshape(1, R), w_main, w_aux, ctab, stab)


def _mla_kv_kernel(x_ref, pe_ref, g_ref, wk_ref, wvt_ref, tab_ref, k_ref, vt_ref, *, Hm):
    a = _rms(x_ref[...].astype(F32), g_ref[...]).astype(BF16)
    kn = _dot(a, wk_ref[...])
    r = pe_ref[...].astype(F32) * tab_ref[...]
    pe = (r + pltpu.roll(r, ROPE_DIM, 1))[:, 0:ROPE_DIM].astype(k_ref.dtype)
    for h in range(Hm):
        k_ref[0, h, :, 0:HEAD_DIM] = kn[:, h * HEAD_DIM:(h + 1) * HEAD_DIM].astype(k_ref.dtype)
        k_ref[0, h, :, HEAD_DIM:QK_DIM] = pe
        vt_ref[0, h, 0:HEAD_DIM, :] = _dot_nt(wvt_ref[h], a).astype(vt_ref.dtype)
        vt_ref[0, h, HEAD_DIM:, :] = jnp.ones((V_PAD, a.shape[0]), vt_ref.dtype)


def mla_keys_values(u, norm_g, w_k, w_vt, tab, B, T, Hm, off):
    R = w_k.shape[0]
    tm = _pick(T, (512, 256, 128))
    nt = T // tm
    oc, op = off["mla_ckv"] // R, off["mla_kpe"] // LANE
    return pl.pallas_call(
        functools.partial(_mla_kv_kernel, Hm=Hm),
        grid=(B * nt,),
        in_specs=[pl.BlockSpec((tm, R), lambda i: (i, oc)),
                  pl.BlockSpec((tm, LANE), lambda i: (i, op)),
                  pl.BlockSpec((1, R), lambda i: (0, 0)),
                  pl.BlockSpec(w_k.shape, lambda i: (0, 0)),
                  pl.BlockSpec(w_vt.shape, lambda i: (0, 0, 0)),
                  pl.BlockSpec((tm, LANE), lambda i: (i % nt, 0))],
        out_specs=[pl.BlockSpec((1, Hm, tm, QK_DIM), lambda i: (i // nt, 0, i % nt, 0)),
                   pl.BlockSpec((1, Hm, HEAD_DIM + V_PAD, tm), lambda i: (i // nt, 0, 0, i % nt))],
        out_shape=[jax.ShapeDtypeStruct((B, Hm, T, QK_DIM), BF16),
                   jax.ShapeDtypeStruct((B, Hm, HEAD_DIM + V_PAD, T), BF16)],
        compiler_params=_params("parallel"),
        name="mla_keys_values",
    )(u, u, norm_g.reshape(1, R), w_k, w_vt, tab)


def _attn_kernel(*refs, nseg, nsub):
    q_ref = refs[0]
    k_refs = refs[1:1 + nseg]
    vt_refs = refs[1 + nseg:1 + 2 * nseg]
    o_ref = refs[1 + 2 * nseg]
    rs = q_ref.shape[2] // nsub

    def scores(i):
        q = q_ref[0, 0, i * rs:(i + 1) * rs, :]
        return [_dot_nt(k[0, 0], q) for k in k_refs]

    def finish(i, s):
        m = functools.reduce(jnp.maximum, [jnp.max(x, axis=0, keepdims=True) for x in s])
        acc = functools.reduce(jnp.add, [_dot(vt[0, 0], jnp.exp2(x - m).astype(BF16)) for x, vt in zip(s, vt_refs)])
        o = acc[0:HEAD_DIM] / acc[HEAD_DIM:HEAD_DIM + 1]
        o_ref[i * rs:(i + 1) * rs, :] = o.T.astype(o_ref.dtype)

    s_next = scores(0)
    for i in range(nsub):
        s_cur = s_next
        if i + 1 < nsub:
            s_next = scores(i + 1)
        finish(i, s_cur)


def attention(q, ks, vts):
    B, Hm, Tq, _ = q.shape
    tq = _pick(Tq, (1024, 512, 256, 128))
    nq = Tq // tq
    nseg = len(ks)
    kspec = lambda a: pl.BlockSpec((1, 1) + a.shape[2:], lambda b, h, i: (b, h, 0, 0))
    return pl.pallas_call(
        functools.partial(_attn_kernel, nseg=nseg, nsub=max(1, tq // 256)),
        grid=(B, Hm, nq),
        in_specs=[pl.BlockSpec((1, 1, tq, QK_DIM), lambda b, h, i: (b, h, i, 0))]
                 + [kspec(a) for a in ks] + [kspec(a) for a in vts],
        out_specs=pl.BlockSpec((tq, HEAD_DIM), lambda b, h, i: (b * nq + i, h)),
        out_shape=jax.ShapeDtypeStruct((B * Tq, Hm * HEAD_DIM), BF16),
        compiler_params=_params("parallel", "parallel", "parallel"),
        name=f"attention_{nseg}seg",
    )(q, *ks, *vts)


def _merge_kernel(ac_ref, ag_ref, am_ref, wc_ref, wg_ref, wm_ref, gc_ref, gg_ref, gm_ref, o_ref):
    sig = lambda r: jax.nn.sigmoid(r[...].astype(F32))
    m = (sig(gc_ref) * _dot(ac_ref[...], wc_ref[0]) + sig(gg_ref) * _dot(ag_ref[...], wg_ref[0])
         + sig(gm_ref) * _dot(am_ref[...], wm_ref[0]))
    o_ref[...] = m.astype(o_ref.dtype)


def merge_branches(ac, ag, am, wc, wg, wm, l, u, off):
    N = ac.shape[0]
    D = wc.shape[2]
    tm = _pick(N, (1024, 512, 256, 128))
    tn = _pick((D, off["gate_conv"], off["gate_gdn"], off["gate_mla"]), (512, 256, 128))
    act = lambda a: pl.BlockSpec((tm, a.shape[1]), lambda i, j: (i, 0))
    wsp = lambda w: pl.BlockSpec((1, w.shape[1], tn), lambda i, j: (l, 0, j))
    gate = lambda name: pl.BlockSpec((tm, tn), lambda i, j: (i, off[name] // tn + j))
    return pl.pallas_call(
        _merge_kernel,
        grid=(N // tm, D // tn),
        in_specs=[act(ac), act(ag), act(am), wsp(wc), wsp(wg), wsp(wm),
                  gate("gate_conv"), gate("gate_gdn"), gate("gate_mla")],
        out_specs=pl.BlockSpec((tm, tn), lambda i, j: (i, j)),
        out_shape=jax.ShapeDtypeStruct((N, D), BF16),
        compiler_params=_params("parallel", "parallel"),
        name="merge_branches",
    )(ac, ag, am, wc, wg, wm, u, u, u)


def _out_proj_kernel(m_ref, w_ref, h_ref, g_ref, o_ref):
    o_ref[...] = h_ref[...] + g_ref[0] * _dot(m_ref[...], w_ref[0])


def out_projection(m, w_o, l, h, gate, rows_per_group):
    N, D = h.shape
    tm = _pick(rows_per_group, (1024, 512, 256, 128))
    tn = _pick(D, (512, 256, 128))
    gdiv = rows_per_group // tm
    return pl.pallas_call(
        _out_proj_kernel,
        grid=(N // tm, D // tn),
        in_specs=[pl.BlockSpec((tm, m.shape[1]), lambda i, j: (i, 0)),
                  pl.BlockSpec((1, m.shape[1], tn), lambda i, j: (l, 0, j)),
                  pl.BlockSpec((tm, tn), lambda i, j: (i, j)),
                  pl.BlockSpec((1, 1, tn), lambda i, j: (i // gdiv, 0, j))],
        out_specs=pl.BlockSpec((tm, tn), lambda i, j: (i, j)),
        out_shape=jax.ShapeDtypeStruct((N, D), F32),
        input_output_aliases={2: 0} if l > 0 else {},
        compiler_params=_params("parallel", "parallel"),
        name="out_projection",
    )(m, w_o, h, gate)


def _router_kernel(x_ref, g_ref, sh_ref, sc_ref, r_ref, hn_ref, lg_ref):
    a = _norm_mod(x_ref[...], g_ref[...], sh_ref[0], sc_ref[0]).astype(BF16)
    hn_ref[...] = a
    lg_ref[0] = _dot_nt(r_ref[0], a)


def router(x, g, sh, sc, router_t, l, B, T, rows_per_group):
    N, D = x.shape
    E = router_t.shape[1]
    tm = _pick(T, (1024, 512, 256, 128))
    nt = T // tm
    gdiv = rows_per_group // tm
    return pl.pallas_call(
        _router_kernel,
        grid=(N // tm,),
        in_specs=[pl.BlockSpec((tm, D), lambda i: (i, 0)),
                  pl.BlockSpec((1, D), lambda i: (0, 0)),
                  pl.BlockSpec((1, 1, D), lambda i: (i // gdiv, 0, 0)),
                  pl.BlockSpec((1, 1, D), lambda i: (i // gdiv, 0, 0)),
                  pl.BlockSpec((1, E, D), lambda i: (l, 0, 0))],
        out_specs=[pl.BlockSpec((tm, D), lambda i: (i, 0)),
                   pl.BlockSpec((1, E, tm), lambda i: (i // nt, 0, i % nt))],
        out_shape=[jax.ShapeDtypeStruct((N, D), BF16), jax.ShapeDtypeStruct((B, E, T), F32)],
        compiler_params=_params("parallel"),
        name="router",
    )(x, g.reshape(1, D), sh, sc, router_t)


def _route_kernel(lg_ref, rm_ref, aff_ref, rmt_ref, *, T, E, cap, W):
    lg = lg_ref[0]
    mx = jnp.max(lg, axis=0, keepdims=True)
    ex = jnp.exp(lg - mx)
    aff = ex / jnp.sum(ex, axis=0, keepdims=True)
    aff_ref[0] = aff
    bits = lax.bitcast_convert_type(aff, jnp.int32)

    def count(mask):
        return jnp.sum(jnp.where(mask, 1.0, 0.0), axis=1, keepdims=True)

    thr = jnp.zeros((E, 1), jnp.int32)
    for bit in range(30, -1, -1):
        cand = thr | (1 << bit)
        thr = jnp.where(count(bits >= cand) >= cap, cand, thr)
    gt = bits > thr
    eq = bits == thr
    need = cap - count(gt)

    ri = lax.broadcasted_iota(jnp.int32, (W, W), 0)
    ci = lax.broadcasted_iota(jnp.int32, (W, W), 1)
    upper = jnp.where(ri < ci, 1.0, 0.0).astype(BF16)
    ident = jnp.where(ri == ci, 1.0, 0.0).astype(BF16)

    def excl_cumsum(mask):
        m = jnp.where(mask, 1.0, 0.0)
        outs, carry = [], jnp.zeros((E, 1), F32)
        for j in range(T // W):
            blk = m[:, j * W:(j + 1) * W]
            outs.append(_dot(blk.astype(BF16), upper) + carry)
            carry = carry + jnp.sum(blk, axis=1, keepdims=True)
        return jnp.concatenate(outs, axis=1) if len(outs) > 1 else outs[0]

    sel = gt | (eq & (excl_cumsum(eq) < need))
    rm = jnp.where(sel, excl_cumsum(sel), -1.0)
    rm_ref[0] = rm
    rmb = rm.astype(BF16)
    for j in range(T // W):
        rmt_ref[0, j * W:(j + 1) * W, :] = _dot_nt(ident, rmb[:, j * W:(j + 1) * W])


def route(logits_t, cap):
    B, E, T = logits_t.shape
    W = 256 if T % 256 == 0 else 128
    blk = pl.BlockSpec((1, E, T), lambda b: (b, 0, 0))
    return pl.pallas_call(
        functools.partial(_route_kernel, T=T, E=E, cap=cap, W=W),
        grid=(B,),
        in_specs=[blk],
        out_specs=[blk, blk, pl.BlockSpec((1, T, E), lambda b: (b, 0, 0))],
        out_shape=[jax.ShapeDtypeStruct((B, E, T), F32), jax.ShapeDtypeStruct((B, E, T), F32),
                   jax.ShapeDtypeStruct((B, T, E), F32)],
        compiler_params=_params("parallel"),
        name="route",
    )(logits_t)


def _gather_kernel(rm_ref, aff_ref, hn_ref, xs_ref, gate_ref, *, cap):
    rm = rm_ref[0, 0]
    T = rm.shape[1]
    slot = lax.broadcasted_iota(jnp.int32, (cap, T), 0).astype(F32)
    hit = slot == rm
    xs_ref[0, 0] = _dot(jnp.where(hit, 1.0, 0.0).astype(BF16), hn_ref[...]).astype(xs_ref.dtype)
    gate_ref[0, 0] = jnp.sum(jnp.where(hit, aff_ref[0, 0], 0.0), axis=1, keepdims=True)


def moe_gather(rm, aff, hn, cap):
    B, E, T = rm.shape
    D = hn.shape[1]
    rowspec = pl.BlockSpec((1, 1, 1, T), lambda b, e: (b, e, 0, 0))
    return pl.pallas_call(
        functools.partial(_gather_kernel, cap=cap),
        grid=(B, E),
        in_specs=[rowspec, rowspec, pl.BlockSpec((T, D), lambda b, e: (b, 0))],
        out_specs=[pl.BlockSpec((1, 1, cap, D), lambda b, e: (b, e, 0, 0)),
                   pl.BlockSpec((1, 1, cap, 1), lambda b, e: (b, e, 0, 0))],
        out_shape=[jax.ShapeDtypeStruct((B, E, cap, D), BF16), jax.ShapeDtypeStruct((B, E, cap, 1), F32)],
        compiler_params=_params("parallel", "parallel"),
        name="moe_gather",
    )(rm.reshape(B, E, 1, T), aff.reshape(B, E, 1, T), hn)


def _ffn_kernel(xs_ref, gate_ref, wg_ref, wu_ref, wd_ref, ye_ref):
    bb, _, cap, D = xs_ref.shape
    xs = xs_ref[...].reshape(bb * cap, D)
    hid = (_silu(_dot(xs, wg_ref[0, 0])) * _dot(xs, wu_ref[0, 0])).astype(BF16)
    ye = _dot(hid, wd_ref[0, 0]) * gate_ref[...].reshape(bb * cap, 1)
    ye_ref[...] = ye.reshape(bb, 1, cap, D).astype(ye_ref.dtype)


def moe_ffn(xs, gate, w_gate, w_up, w_down, l):
    B, E, cap, D = xs.shape
    FF = w_gate.shape[3]
    bb = max(1, min(B, 512 // cap))
    while B % bb:
        bb -= 1
    tok = lambda last: pl.BlockSpec((bb, 1, cap, last), lambda e, b: (b, e, 0, 0))
    return pl.pallas_call(
        _ffn_kernel,
        grid=(E, B // bb),
        in_specs=[tok(D), tok(1),
                  pl.BlockSpec((1, 1, D, FF), lambda e, b: (l, e, 0, 0)),
                  pl.BlockSpec((1, 1, D, FF), lambda e, b: (l, e, 0, 0)),
                  pl.BlockSpec((1, 1, FF, D), lambda e, b: (l, e, 0, 0))],
        out_specs=tok(D),
        out_shape=jax.ShapeDtypeStruct((B, E, cap, D), BF16),
        compiler_params=_params("parallel", "parallel"),
        name="moe_ffn",
    )(xs, gate, w_gate, w_up, w_down)


def _scatter_kernel(rmt_ref, ye_ref, h_ref, g_ref, o_ref, *, E, cap):
    tt = rmt_ref.shape[1]
    rmt = rmt_ref[0]
    slot = lax.broadcasted_iota(jnp.int32, (tt, cap), 1).astype(F32)
    acc = jnp.zeros(o_ref.shape, F32)
    for e in range(E):
        onehot = jnp.where(rmt[:, e:e + 1] == slot, 1.0, 0.0).astype(BF16)
        acc = acc + _dot(onehot, ye_ref[0, e * cap:(e + 1) * cap, :])
    o_ref[...] = h_ref[...] + g_ref[0] * acc


def moe_scatter(rmt, ye, h, gate, T):
    B, _, E = rmt.shape
    _, EC, D = ye.shape
    cap = EC // E
    tt = _pick(T, (512, 256, 128))
    tn = _pick(D, (1024, 512, 256, 128))
    nt = T // tt
    return pl.pallas_call(
        functools.partial(_scatter_kernel, E=E, cap=cap),
        grid=(B, D // tn, nt),
        in_specs=[pl.BlockSpec((1, tt, E), lambda b, j, i: (b, i, 0)),
                  pl.BlockSpec((1, EC, tn), lambda b, j, i: (b, 0, j)),
                  pl.BlockSpec((tt, tn), lambda b, j, i: (b * nt + i, j)),
                  pl.BlockSpec((1, 1, tn), lambda b, j, i: (b, 0, j))],
        out_specs=pl.BlockSpec((tt, tn), lambda b, j, i: (b * nt + i, j)),
        out_shape=jax.ShapeDtypeStruct(h.shape, F32),
        input_output_aliases={2: 0},
        compiler_params=_params("parallel", "parallel", "parallel"),
        name="moe_scatter",
    )(rmt, ye, h, gate)


def expert_choice_moe(h, g, sh, sc, gate, router_t, w_gate, w_up, w_down, l, B, T, rows_per_group):
    E = router_t.shape[1]
    cap = EC_CAPACITY * T // E
    assert cap <= 256 and cap % 16 == 0, "slot indices must stay exact in bf16 and fill bf16 sublane tiles"
    hn, logits_t = router(h, g, sh, sc, router_t, l, B, T, rows_per_group)
    rm, aff, rmt = route(logits_t, cap)
    xs, gsl = moe_gather(rm, aff, hn, cap)
    ye = moe_ffn(xs, gsl, w_gate, w_up, w_down, l)
    return moe_scatter(rmt, ye.reshape(B, E * cap, ye.shape[-1]), h, gate, T)


def _final_norm_kernel(x_ref, g_ref, o_ref):
    o_ref[...] = _rms(x_ref[...], g_ref[...])


def final_norm(x, g):
    N, D = x.shape
    tm = _pick(N, (512, 256, 128))
    return pl.pallas_call(
        _final_norm_kernel,
        grid=(N // tm,),
        in_specs=[pl.BlockSpec((tm, D), lambda i: (i, 0)), pl.BlockSpec((1, D), lambda i: (0, 0))],
        out_specs=pl.BlockSpec((tm, D), lambda i: (i, 0)),
        out_shape=jax.ShapeDtypeStruct((N, D), F32),
        compiler_params=_params("parallel"),
        name="final_norm",
    )(x, g.reshape(1, D))


def _swap_rope(w):
    half = ROPE_DIM // 2
    return jnp.concatenate([-w[..., half:], w[..., :half]], axis=-1)


_PACKED_ORDER = ("mla_cq", "gdn_small", "mla_kpe", "mla_ckv", "conv_x", "conv_b", "conv_c", "gdn_q", "gdn_k", "gdn_v",
                 "gdn_z", "gate_conv", "gate_gdn", "gate_mla")


def _in_layout(C, Hg, RQ, RKV, D):
    width = {"mla_cq": RQ, "gdn_small": LANE, "mla_kpe": LANE, "mla_ckv": RKV, "conv_x": C, "conv_b": C, "conv_c": C,
             "gdn_q": Hg * HEAD_DIM, "gdn_k": Hg * HEAD_DIM, "gdn_v": Hg * HEAD_DIM, "gdn_z": Hg * HEAD_DIM,
             "gate_conv": D, "gate_gdn": D, "gate_mla": D}
    off, o = {}, 0
    for name in _PACKED_ORDER:
        off[name] = o
        o += width[name]
    assert off["mla_cq"] % RQ == 0 and off["mla_ckv"] % RKV == 0
    return off, o


def _pack_w_in(w, C, Hg, RQ, RKV, D):
    names = ("conv_x", "conv_b", "conv_c", "gdn_q", "gdn_k", "gdn_v", "gdn_z", "gdn_small", "mla_cq", "mla_ckv",
             "mla_kpe", "gate_conv", "gate_gdn", "gate_mla")
    sizes = (C, C, C, Hg * HEAD_DIM, Hg * HEAD_DIM, Hg * HEAD_DIM, Hg * HEAD_DIM, 4 * Hg, RQ, RKV, ROPE_DIM, D, D, D)
    w = w.astype(BF16)
    seg, o = {}, 0
    for name, size in zip(names, sizes):
        seg[name] = w[..., o:o + size]
        o += size
    seg["gdn_small"] = jnp.pad(seg["gdn_small"], ((0, 0), (0, 0), (0, LANE - 4 * Hg)))
    seg["mla_kpe"] = jnp.concatenate([seg["mla_kpe"], _swap_rope(seg["mla_kpe"])], axis=-1)
    return jnp.concatenate([seg[n] for n in _PACKED_ORDER], axis=-1)


def _rope_tables(T):
    rows = T // GRID_W
    row = jnp.repeat(jnp.arange(rows), GRID_W)
    col = jnp.tile(jnp.arange(GRID_W), rows)
    n_freq = ROPE_DIM // 4
    inv_freq = 1.0 / (ROPE_THETA ** (jnp.arange(n_freq, dtype=F32) / n_freq))
    ang = jnp.concatenate([row[:, None] * inv_freq, col[:, None] * inv_freq], axis=-1)
    cos, sin = jnp.cos(ang), jnp.sin(ang)
    return jnp.concatenate([cos, cos], axis=1), jnp.concatenate([sin, sin], axis=1)


def kernel(x, c, ctx, c_ctx, w_ada, b_ada, norm1_g, w_in, conv_w, conv_out, gdn_conv_w, gdn_a_log, gdn_dt_bias,
           gdn_norm_g, gdn_out, mla_q_norm_g, mla_w_uq, mla_kv_norm_g, mla_w_ukv, mla_out, w_o, norm2_g, router_w,
           w_gate, w_up, w_down, final_g):
    B, T, D = x.shape
    TC = ctx.shape[1]
    L = w_ada.shape[0]
    C = conv_w.shape[-1]
    Hg = gdn_a_log.shape[-1]
    RQ = mla_q_norm_g.shape[-1]
    RKV = mla_kv_norm_g.shape[-1]
    Hm = mla_w_uq.shape[-1] // QK_DIM
    assert 4 * Hg <= LANE and T % GDN_CHUNK == 0 and TC % GDN_CHUNK == 0

    off, _ = _in_layout(C, Hg, RQ, RKV, D)

    R = -(-(B + 1) // 8) * 8
    cc = jnp.concatenate([c, c_ctx[None, :], jnp.zeros((R - B - 1, D), F32)], axis=0)
    mod = ada_modulation(cc, w_ada, b_ada)

    cos64, sin64 = _rope_tables(T)
    q_ct, q_st = jnp.tile(cos64, (1, Hm)), jnp.tile(sin64, (1, Hm))
    q_ct0, q_st0 = jnp.ones((TC, Hm * ROPE_DIM), F32), jnp.zeros((TC, Hm * ROPE_DIM), F32)
    k_tab = jnp.concatenate([cos64, sin64], 1)
    k_tab0 = jnp.concatenate([jnp.ones((TC, ROPE_DIM), F32), jnp.zeros((TC, ROPE_DIM), F32)], 1)

    h = x.reshape(B * T, D)
    hc = ctx.reshape(B * TC, D)
    zero_state = jnp.zeros((B, Hg, HEAD_DIM, HEAD_DIM), F32)

    w_in_p = _pack_w_in(w_in, C, Hg, RQ, RKV, D)
    wc_o, wg_o, wm_o, wo_b = (a.astype(BF16) for a in (conv_out, gdn_out, mla_out, w_o))
    router_t = jnp.swapaxes(router_w, 1, 2).astype(BF16)
    wgt, wup, wdn = w_gate.astype(BF16), w_up.astype(BF16), w_down.astype(BF16)

    for l in range(L):
        need_ctx = l < L - 1
        wq = mla_w_uq[l].reshape(RQ, Hm, QK_DIM)
        wq_pe = wq[..., HEAD_DIM:]
        wq_main = jnp.concatenate([wq[..., :HEAD_DIM].reshape(RQ, -1), wq_pe.reshape(RQ, -1)], 1).astype(BF16)
        wq_aux = _swap_rope(wq_pe).reshape(RQ, -1).astype(BF16)
        wkv = mla_w_ukv[l].reshape(RKV, Hm, 2 * HEAD_DIM)
        w_k = wkv[..., :HEAD_DIM].reshape(RKV, -1).astype(BF16)
        w_vt = wkv[..., HEAD_DIM:].transpose(1, 2, 0).astype(BF16)
        pad = jnp.zeros((LANE - 2 * Hg,), F32)
        alog_row = jnp.concatenate([gdn_a_log[l].reshape(-1), pad]).reshape(1, LANE)
        dtb_row = jnp.concatenate([gdn_dt_bias[l].reshape(-1), pad]).reshape(1, LANE)

        ml = mod[l, :B].reshape(B, 1, 6 * D)
        mc = mod[l, B:B + 1].reshape(1, 1, 6 * D)
        sh1, sc1, g1, sh2, sc2, g2 = (ml[..., i * D:(i + 1) * D] for i in range(6))
        csh1, csc1, cg1, csh2, csc2, cg2 = (mc[..., i * D:(i + 1) * D] for i in range(6))

        u_lat = in_projection(h, norm1_g[l], sh1, sc1, w_in_p, l, T)
        u_ctx = in_projection(hc, norm1_g[l], csh1, csc1, w_in_p, l, B * TC)

        ag_ctx, s_f, s_b = gdn_mixer(u_ctx, gdn_conv_w[l], alog_row, dtb_row, gdn_norm_g[l], zero_state, zero_state,
                                     B, TC, Hg, off)
        ag_lat, _, _ = gdn_mixer(u_lat, gdn_conv_w[l], alog_row, dtb_row, gdn_norm_g[l], s_f, s_b, B, T, Hg, off)

        k_ctx, v_ctx = mla_keys_values(u_ctx, mla_kv_norm_g[l], w_k, w_vt, k_tab0, B, TC, Hm, off)
        k_lat, v_lat = mla_keys_values(u_lat, mla_kv_norm_g[l], w_k, w_vt, k_tab, B, T, Hm, off)
        q_lat = mla_queries(u_lat, mla_q_norm_g[l], wq_main, wq_aux, q_ct, q_st, B, T, Hm, off)
        am_lat = attention(q_lat, [k_ctx, k_lat], [v_ctx, v_lat])

        ac_lat = conv_mixer_front(u_lat, conv_w[l], B, T, off)
        m_lat = merge_branches(ac_lat, ag_lat, am_lat, wc_o, wg_o, wm_o, l, u_lat, off)
        h = out_projection(m_lat, wo_b, l, h, g1, T)
        h = expert_choice_moe(h, norm2_g[l], sh2, sc2, g2, router_t, wgt, wup, wdn, l, B, T, T)

        if need_ctx:
            q_ctx = mla_queries(u_ctx, mla_q_norm_g[l], wq_main, wq_aux, q_ct0, q_st0, B, TC, Hm, off)
            am_ctx = attention(q_ctx, [k_ctx], [v_ctx])
            ac_ctx = conv_mixer_front(u_ctx, conv_w[l], B, TC, off)
            m_ctx = merge_branches(ac_ctx, ag_ctx, am_ctx, wc_o, wg_o, wm_o, l, u_ctx, off)
            hc = out_projection(m_ctx, wo_b, l, hc, cg1, B * TC)
            cg2b = jnp.broadcast_to(cg2, (B, 1, D))
            hc = expert_choice_moe(hc, norm2_g[l], csh2, csc2, cg2b, router_t, wgt, wup, wdn, l, B, TC, B * TC)

    return final_norm(h, final_g).reshape(B, T, D)
```

```python
import functools
import math

import jax
import jax.numpy as jnp
from jax import lax
from jax.experimental import pallas as pl
from jax.experimental.pallas import tpu as pltpu

F32 = jnp.float32
BF16 = jnp.bfloat16

RMS_EPS = 1e-6
L2_EPS = 1e-6
ROPE_THETA = 10000.0
GRID_W = 64
HEAD_DIM = 128
ROPE_DIM = 64
QK_DIM = HEAD_DIM + ROPE_DIM
GDN_CHUNK = 128
GDN_SUB = 16
EC_CAPACITY = 2
LANE = 128
V_PAD = 16
ATTN_LOGIT_SCALE = QK_DIM ** -0.5 * math.log2(math.e)
VMEM_LIMIT = 56 * 1024 * 1024


def _pick(n, cands):
    ns = n if isinstance(n, tuple) else (n,)
    for c in cands:
        if all(v % c == 0 for v in ns):
            return c
    raise ValueError(f"no tile in {cands} divides {ns}")


def _params(*sem):
    return pltpu.CompilerParams(dimension_semantics=sem, vmem_limit_bytes=VMEM_LIMIT)


def _dot(a, b):
    return jnp.dot(a, b, preferred_element_type=F32)


def _dot_nt(a, b):
    return lax.dot_general(a, b, (((1,), (1,)), ((), ())), preferred_element_type=F32)


def _dot_tn(a, b):
    return lax.dot_general(a, b, (((0,), (0,)), ((), ())), preferred_element_type=F32)


def _split2(x):
    hi = x.astype(BF16)
    lo = (x - hi.astype(F32)).astype(BF16)
    return hi, lo


def _split3(x):
    hi = x.astype(BF16)
    r = x - hi.astype(F32)
    mid = r.astype(BF16)
    lo = (r - mid.astype(F32)).astype(BF16)
    return hi, mid, lo


def _silu(x):
    return x * jax.nn.sigmoid(x)


def _softplus(x):
    return jnp.maximum(x, 0.0) + jnp.log(1.0 + jnp.exp(-jnp.abs(x)))


def _ada_kernel(c_ref, w_ref, b_ref, o_ref):
    a = _silu(c_ref[...])
    ah, al = _split2(a)
    wh, wl = _split2(w_ref[0])
    o_ref[0] = _dot(ah, wh) + (_dot(ah, wl) + _dot(al, wh)) + b_ref[0]


def ada_modulation(cc, w_ada, b_ada):
    L, D, D6 = w_ada.shape
    R = cc.shape[0]
    tn = _pick(D6, (1024, 512, 256, 128))
    return pl.pallas_call(
        _ada_kernel,
        grid=(L, D6 // tn),
        in_specs=[pl.BlockSpec((R, D), lambda l, j: (0, 0)),
                  pl.BlockSpec((1, D, tn), lambda l, j: (l, 0, j)),
                  pl.BlockSpec((1, 1, tn), lambda l, j: (l, 0, j))],
        out_specs=pl.BlockSpec((1, R, tn), lambda l, j: (l, 0, j)),
        out_shape=jax.ShapeDtypeStruct((L, R, D6), F32),
        compiler_params=_params("parallel", "parallel"),
        name="ada_modulation",
    )(cc, w_ada, b_ada.reshape(L, 1, D6))


def _norm_mod(x, g, sh, sc):
    ms = jnp.mean(x * x, axis=-1, keepdims=True)
    y = x * lax.rsqrt(ms + RMS_EPS) * g
    return y * (1.0 + sc) + sh


def _norm_mod_kernel(x_ref, g_ref, sh_ref, sc_ref, o_ref):
    o_ref[...] = _norm_mod(x_ref[...], g_ref[...], sh_ref[0], sc_ref[0]).astype(o_ref.dtype)


def norm_modulate(x, g, sh, sc, rows_per_group):
    N, D = x.shape
    tm = _pick(rows_per_group, (512, 256, 128))
    gdiv = rows_per_group // tm
    return pl.pallas_call(
        _norm_mod_kernel,
        grid=(N // tm,),
        in_specs=[pl.BlockSpec((tm, D), lambda i: (i, 0)),
                  pl.BlockSpec((1, D), lambda i: (0, 0)),
                  pl.BlockSpec((1, 1, D), lambda i: (i // gdiv, 0, 0)),
                  pl.BlockSpec((1, 1, D), lambda i: (i // gdiv, 0, 0))],
        out_specs=pl.BlockSpec((tm, D), lambda i: (i, 0)),
        out_shape=jax.ShapeDtypeStruct((N, D), BF16),
        compiler_params=_params("parallel"),
        name="norm_modulate",
    )(x, g.reshape(1, D), sh, sc)


def _in_proj_kernel(a_ref, w_ref, o_ref):
    o_ref[...] = _dot(a_ref[...], w_ref[0]).astype(o_ref.dtype)


def in_projection(a, w, l):
    N, D = a.shape
    NC = w.shape[2]
    tm = _pick(N, (2048, 1024, 512, 256, 128))
    tn = _pick(NC, (512, 256, 128))
    return pl.pallas_call(
        _in_proj_kernel,
        grid=(N // tm, NC // tn),
        in_specs=[pl.BlockSpec((tm, D), lambda i, j: (i, 0)),
                  pl.BlockSpec((1, D, tn), lambda i, j: (l, 0, j))],
        out_specs=pl.BlockSpec((tm, tn), lambda i, j: (i, j)),
        out_shape=jax.ShapeDtypeStruct((N, NC), BF16),
        compiler_params=_params("parallel", "parallel"),
        name="in_projection",
    )(a, w)


def _conv3(x, w, T):
    row = lax.broadcasted_iota(jnp.int32, x.shape, 0)
    prev = jnp.where(row == 0, 0.0, pltpu.roll(x, 1, 0))
    nxt = jnp.where(row == T - 1, 0.0, pltpu.roll(x, T - 1, 0))
    return prev * w[0:1] + x * w[1:2] + nxt * w[2:3]


def _conv_mixer_kernel(x_ref, b_ref, c_ref, w_ref, o_ref, *, T):
    v = c_ref[...].astype(F32) * x_ref[...].astype(F32)
    o_ref[...] = (b_ref[...].astype(F32) * _conv3(v, w_ref[...], T)).astype(o_ref.dtype)


def conv_mixer_front(u, conv_w, B, T, off):
    C = conv_w.shape[1]
    tc = _pick(C, (256, 128))
    ox, ob, oc = (off[k] // tc for k in ("conv_x", "conv_b", "conv_c"))
    return pl.pallas_call(
        functools.partial(_conv_mixer_kernel, T=T),
        grid=(B, C // tc),
        in_specs=[pl.BlockSpec((T, tc), lambda b, j: (b, ox + j)),
                  pl.BlockSpec((T, tc), lambda b, j: (b, ob + j)),
                  pl.BlockSpec((T, tc), lambda b, j: (b, oc + j)),
                  pl.BlockSpec((3, tc), lambda b, j: (0, j))],
        out_specs=pl.BlockSpec((T, tc), lambda b, j: (b, j)),
        out_shape=jax.ShapeDtypeStruct((B * T, C), BF16),
        compiler_params=_params("parallel", "parallel"),
        name="conv_mixer_front",
    )(u, u, u, conv_w)


def _gdn_solve(a_list, rhs_list, sub_mask, eye):
    g = a_list[0].shape[0]
    b16 = lambda ts: [t.astype(BF16) for t in ts]
    dots = lambda xs, ys: [_dot(x, y) for x, y in zip(xs, ys)]
    d = [jnp.where(sub_mask, a, 0.0) for a in a_list]
    n = [a - dd for a, dd in zip(a_list, d)]
    db = b16(d)
    d2b = b16(dots(db, db))
    d4b = b16(dots(d2b, d2b))
    d8b = b16(dots(d4b, d4b))
    x = [eye - dd for dd in d]
    for p in (d2b, d4b, d8b):
        x = [xx + t for xx, t in zip(x, dots(b16(x), p))]
    z = dots(b16(x), b16([jnp.concatenate([nn, r], axis=1) for nn, r in zip(n, rhs_list)]))
    mb = b16([zz[:, :g] for zz in z])
    zr = [zz[:, g:] for zz in z]
    powers = [mb]
    for _ in range(int(math.log2(GDN_CHUNK // GDN_SUB)) - 1):
        powers.append(b16(dots(powers[-1], powers[-1])))
    for p in reversed(powers[1:]):
        zr = [r + t for r, t in zip(zr, dots(p, b16(zr)))]
    return [r - t for r, t in zip(zr, dots(mb, b16(zr)))]


def _gdn_kernel(q_ref, k_ref, v_ref, z_ref, ab_ref, wq_ref, wk_ref, wv_ref, alog_ref, dtb_ref, ng_ref,
                s0f_ref, s0b_ref, o_ref, sf_ref, sb_ref,
                q_s, k_s, v_s, col_s, u_s, w_s, qg_s, kd_s, in_s, eg_s, o_s, st_s, *, T, H, G, GI, HB):
    hb = pl.program_id(1)
    NC = T // GDN_CHUNK
    L = GDN_CHUNK

    def l2n(x):
        return x * lax.rsqrt(jnp.sum(x * x, axis=-1, keepdims=True) + L2_EPS)

    ri = lax.broadcasted_iota(jnp.int32, (G, G), 0)
    ci = lax.broadcasted_iota(jnp.int32, (G, G), 1)
    same_chunk = (ri >> int(math.log2(L))) == (ci >> int(math.log2(L)))
    same_sub = (ri >> int(math.log2(GDN_SUB))) == (ci >> int(math.log2(GDN_SUB)))
    eye = jnp.where(ri == ci, 1.0, 0.0).astype(F32)
    ones_chunk = jnp.where(same_chunk, 1.0, 0.0).astype(BF16)

    ab = ab_ref[...].astype(F32)
    lane = lax.broadcasted_iota(jnp.int32, ab.shape, 1)
    gfull = -jnp.exp(alog_ref[...]) * _softplus(ab + dtb_ref[...])
    bfull = jax.nn.sigmoid(ab)

    def col(x, idx):
        return jnp.sum(jnp.where(lane == idx, x, 0.0), axis=1, keepdims=True)

    for hh in range(HB):
        hsl = slice(hh * HEAD_DIM, (hh + 1) * HEAD_DIM)
        head = hb * HB + hh

        def prep(x_ref, w_ref):
            return _silu(_conv3(x_ref[:, hsl].astype(F32), w_ref[:, hsl], T))

        q_s[hh] = l2n(prep(q_ref, wq_ref)) * (HEAD_DIM ** -0.5)
        k_s[hh] = l2n(prep(k_ref, wk_ref))
        v_s[hh] = prep(v_ref, wv_ref)

        g_f, g_b = col(gfull, head), col(gfull, H + head)
        b_f, b_b = col(bfull, 2 * H + head), col(bfull, 3 * H + head)
        col_s[hh] = jnp.where(lane == 0, g_f, jnp.where(lane == 1, g_b, jnp.where(lane == 2, b_f, b_b)))
        st_s[hh, 0] = s0f_ref[0, hh]
        st_s[hh, 1] = s0b_ref[0, hh]

    chains = [(hh, d) for hh in range(HB) for d in range(2)]
    incl = [same_chunk & (ci <= ri), same_chunk & (ci >= ri)]
    strict = [same_chunk & (ci < ri), same_chunk & (ci > ri)]
    incl_b = [jnp.where(m, 1.0, 0.0).astype(BF16) for m in incl]
    ones_8g = jnp.ones((8, G), BF16)

    def exact3_list(ms, xs):
        parts = [_split3(x) for x in xs]
        return [_dot(m, p[0]) + _dot(m, p[1]) + _dot(m, p[2]) for m, p in zip(ms, parts)]

    def group_body(gi, carry):
        units = [(j, hh) for j in range(GI) for hh in range(HB)]
        gch = [(ui, d) for ui in range(len(units)) for d in range(2)]
        r0 = [pl.multiple_of((gi * GI + j) * G, G) for j in range(GI)]
        rows = [pl.ds(r0[j], G) for j, _ in units]
        qg = [q_s[hh, r, :] for (_, hh), r in zip(units, rows)]
        kg = [k_s[hh, r, :] for (_, hh), r in zip(units, rows)]
        vg = [v_s[hh, r, :] for (_, hh), r in zip(units, rows)]
        cols = [col_s[hh, r, :] for (_, hh), r in zip(units, rows)]
        kb16 = [k.astype(BF16) for k in kg]
        kk = [_dot_nt(kb, kb) for kb in kb16]
        qk = [_dot_nt(q.astype(BF16), kb) for q, kb in zip(qg, kb16)]
        bcol = [cols[ui][:, 2 + d:3 + d] for ui, d in gch]
        gb = [jnp.broadcast_to(cols[ui][:, d:d + 1], (G, LANE)) for ui, d in gch]
        gc = exact3_list([incl_b[d] for _, d in gch], gb)
        gl = exact3_list([ones_chunk] * len(gch), gb)
        cmat = [jnp.concatenate([g] * (G // LANE), axis=1) if G > LANE else g for g in gc]
        rrow = [r[0:1] for r in exact3_list([ones_8g] * len(gch),
                                            [jnp.where(ri == ci, cm, 0.0) for cm in cmat])]
        decay = [jnp.where(incl[d], jnp.exp(jnp.where(incl[d], cm - rr, 0.0)), 0.0)
                 for (_, d), cm, rr in zip(gch, cmat, rrow)]
        a = [jnp.where(strict[d], bc * kk[ui] * dc, 0.0) for (ui, d), bc, dc in zip(gch, bcol, decay)]
        egc = [jnp.exp(g) for g in gc]
        rhs = [jnp.concatenate([vg[ui] * bc, kg[ui] * bc * e], axis=1) for (ui, _), bc, e in zip(gch, bcol, egc)]
        uw = _gdn_solve(a, rhs, same_sub, eye)
        for i, (ui, d) in enumerate(gch):
            j, hh = units[ui]
            u_s[hh, d, rows[ui], :] = uw[i][:, :HEAD_DIM]
            w_s[hh, d, rows[ui], :] = uw[i][:, HEAD_DIM:].astype(BF16)
            qg_s[hh, d, rows[ui], :] = (qg[ui] * egc[i]).astype(BF16)
            kd_s[hh, d, rows[ui], :] = (kg[ui] * jnp.exp(gl[i] - gc[i])).astype(BF16)
            egl = jnp.exp(gl[i])
            intra = (qk[ui] * decay[i]).astype(BF16)
            for c in range(G // L):
                e0 = pl.multiple_of(((gi * GI + j) * (G // L) + c) * 8, 8)
                eg_s[hh, d, pl.ds(e0, 8), :] = egl[c * L:c * L + 8]
                in_s[hh, d, pl.ds(r0[j] + c * L, L), :] = intra[c * L:(c + 1) * L, c * L:(c + 1) * L]
        return carry

    lax.fori_loop(0, T // (G * GI), group_body, 0)

    def scan_body(c, carry):
        rws = [pl.ds(pl.multiple_of((c if d == 0 else NC - 1 - c) * L, L), L) for _, d in chains]
        egr = [pl.ds(pl.multiple_of((c if d == 0 else NC - 1 - c) * 8, 8), 8) for _, d in chains]
        s = [st_s[hh, d] for hh, d in chains]
        wq = [jnp.concatenate([w_s[hh, d, r, :], qg_s[hh, d, r, :]], axis=0) for (hh, d), r in zip(chains, rws)]
        ws = [_dot(x, y.astype(BF16)) for x, y in zip(wq, s)]
        vb = [(u_s[hh, d, r, :] - w[:L]).astype(BF16) for (hh, d), r, w in zip(chains, rws, ws)]
        oi = [_dot(in_s[hh, d, r, :], v) for (hh, d), r, v in zip(chains, rws, vb)]
        ds_ = [_dot_tn(kd_s[hh, d, r, :], v) for (hh, d), r, v in zip(chains, rws, vb)]
        for i, (hh, d) in enumerate(chains):
            o_s[hh, d, rws[i], :] = ws[i][L:] + oi[i]
            st_s[hh, d] = s[i] * eg_s[hh, d, egr[i], :][0:1] + ds_[i]
        return carry

    lax.fori_loop(0, NC, scan_body, 0)

    for hh in range(HB):
        hsl = slice(hh * HEAD_DIM, (hh + 1) * HEAD_DIM)
        sf_ref[0, hh] = st_s[hh, 0]
        sb_ref[0, hh] = st_s[hh, 1]
        o = o_s[hh, 0] + o_s[hh, 1]
        y = o * lax.rsqrt(jnp.mean(o * o, axis=-1, keepdims=True) + RMS_EPS) * ng_ref[...]
        o_ref[:, hsl] = (y * _silu(z_ref[:, hsl].astype(F32))).astype(o_ref.dtype)


def gdn_mixer(u, conv_w, alog_row, dtb_row, norm_g, s0f, s0b, B, T, H, off):
    G = GDN_CHUNK
    assert G == LANE
    GI = 2 if T % (2 * G) == 0 else 1
    HB = 2 if H % 2 == 0 else 1
    W = HB * HEAD_DIM
    assert all(off[k] % W == 0 for k in ("gdn_q", "gdn_k", "gdn_v", "gdn_z"))
    oq, ok, ov, oz = (off[k] // W for k in ("gdn_q", "gdn_k", "gdn_v", "gdn_z"))
    oab = off["gdn_small"] // LANE
    seq = lambda o: pl.BlockSpec((T, W), lambda b, h: (b, o + h))
    cw = lambda o: pl.BlockSpec((3, W), lambda b, h: (0, o * (H // HB) + h))
    row = pl.BlockSpec((1, LANE), lambda b, h: (0, 0))
    st = pl.BlockSpec((1, HB, HEAD_DIM, HEAD_DIM), lambda b, h: (b, h, 0, 0))
    tbuf = lambda: pltpu.VMEM((HB, T, HEAD_DIM), F32)
    hbuf = lambda dt: pltpu.VMEM((HB, 2, T, HEAD_DIM), dt)
    return pl.pallas_call(
        functools.partial(_gdn_kernel, T=T, H=H, G=G, GI=GI, HB=HB),
        grid=(B, H // HB),
        in_specs=[seq(oq), seq(ok), seq(ov), seq(oz),
                  pl.BlockSpec((T, LANE), lambda b, h: (b, oab)),
                  cw(0), cw(1), cw(2), row, row, row, st, st],
        out_specs=[pl.BlockSpec((T, W), lambda b, h: (b, h)), st, st],
        out_shape=[jax.ShapeDtypeStruct((B * T, H * HEAD_DIM), BF16),
                   jax.ShapeDtypeStruct((B, H, HEAD_DIM, HEAD_DIM), F32),
                   jax.ShapeDtypeStruct((B, H, HEAD_DIM, HEAD_DIM), F32)],
        scratch_shapes=[tbuf(), tbuf(), tbuf(), tbuf(), hbuf(F32), hbuf(BF16), hbuf(BF16), hbuf(BF16),
                        pltpu.VMEM((HB, 2, T, GDN_CHUNK), BF16), pltpu.VMEM((HB, 2, T // 8, HEAD_DIM), F32),
                        hbuf(F32), pltpu.VMEM((HB, 2, HEAD_DIM, HEAD_DIM), F32)],
        compiler_params=_params("parallel", "parallel"),
        name="gdn_mixer",
    )(u, u, u, u, u, conv_w, conv_w, conv_w, alog_row, dtb_row, norm_g.reshape(1, HEAD_DIM), s0f, s0b)


def _rms(x, g):
    return x * lax.rsqrt(jnp.mean(x * x, axis=-1, keepdims=True) + RMS_EPS) * g


def _mla_q_kernel(x_ref, g_ref, wm_ref, wa_ref, ct_ref, st_ref, o_ref, *, Hm):
    a = _rms(x_ref[...].astype(F32), g_ref[...]).astype(BF16)
    main = _dot(a, wm_ref[...]) * ATTN_LOGIT_SCALE
    nn = Hm * HEAD_DIM
    pe = main[:, nn:] * ct_ref[...] + (_dot(a, wa_ref[...]) * ATTN_LOGIT_SCALE) * st_ref[...]
    for h in range(Hm):
        o_ref[0, h, :, 0:HEAD_DIM] = main[:, h * HEAD_DIM:(h + 1) * HEAD_DIM].astype(o_ref.dtype)
        o_ref[0, h, :, HEAD_DIM:QK_DIM] = pe[:, h * ROPE_DIM:(h + 1) * ROPE_DIM].astype(o_ref.dtype)


def mla_queries(u, norm_g, w_main, w_aux, ctab, stab, B, T, Hm, off):
    R = w_main.shape[0]
    tm = _pick(T, (512, 256, 128))
    nt = T // tm
    oc = off["mla_cq"] // R
    full = lambda a: pl.BlockSpec(a.shape, lambda i: (0, 0))
    tab = pl.BlockSpec((tm, Hm * ROPE_DIM), lambda i: (i % nt, 0))
    return pl.pallas_call(
        functools.partial(_mla_q_kernel, Hm=Hm),
        grid=(B * nt,),
        in_specs=[pl.BlockSpec((tm, R), lambda i: (i, oc)), pl.BlockSpec((1, R), lambda i: (0, 0)),
                  full(w_main), full(w_aux), tab, tab],
        out_specs=pl.BlockSpec((1, Hm, tm, QK_DIM), lambda i: (i // nt, 0, i % nt, 0)),
        out_shape=jax.ShapeDtypeStruct((B, Hm, T, QK_DIM), BF16),
        compiler_params=_params("parallel"),
        name="mla_queries",
    )(u, norm_g.reshape(1, R), w_main, w_aux, ctab, stab)


def _mla_kv_kernel(x_ref, pe_ref, g_ref, wk_ref, wvt_ref, tab_ref, k_ref, vt_ref, *, Hm):
    a = _rms(x_ref[...].astype(F32), g_ref[...]).astype(BF16)
    kn = _dot(a, wk_ref[...])
    r = pe_ref[...].astype(F32) * tab_ref[...]
    pe = (r + pltpu.roll(r, ROPE_DIM, 1))[:, 0:ROPE_DIM].astype(k_ref.dtype)
    for h in range(Hm):
        k_ref[0, h, :, 0:HEAD_DIM] = kn[:, h * HEAD_DIM:(h + 1) * HEAD_DIM].astype(k_ref.dtype)
        k_ref[0, h, :, HEAD_DIM:QK_DIM] = pe
        vt_ref[0, h, 0:HEAD_DIM, :] = _dot_nt(wvt_ref[h], a).astype(vt_ref.dtype)
        vt_ref[0, h, HEAD_DIM:, :] = jnp.ones((V_PAD, a.shape[0]), vt_ref.dtype)


def mla_keys_values(u, norm_g, w_k, w_vt, tab, B, T, Hm, off):
    R = w_k.shape[0]
    tm = _pick(T, (512, 256, 128))
    nt = T // tm
    oc, op = off["mla_ckv"] // R, off["mla_kpe"] // LANE
    return pl.pallas_call(
        functools.partial(_mla_kv_kernel, Hm=Hm),
        grid=(B * nt,),
        in_specs=[pl.BlockSpec((tm, R), lambda i: (i, oc)),
                  pl.BlockSpec((tm, LANE), lambda i: (i, op)),
                  pl.BlockSpec((1, R), lambda i: (0, 0)),
                  pl.BlockSpec(w_k.shape, lambda i: (0, 0)),
                  pl.BlockSpec(w_vt.shape, lambda i: (0, 0, 0)),
                  pl.BlockSpec((tm, LANE), lambda i: (i % nt, 0))],
        out_specs=[pl.BlockSpec((1, Hm, tm, QK_DIM), lambda i: (i // nt, 0, i % nt, 0)),
                   pl.BlockSpec((1, Hm, HEAD_DIM + V_PAD, tm), lambda i: (i // nt, 0, 0, i % nt))],
        out_shape=[jax.ShapeDtypeStruct((B, Hm, T, QK_DIM), BF16),
                   jax.ShapeDtypeStruct((B, Hm, HEAD_DIM + V_PAD, T), BF16)],
        compiler_params=_params("parallel"),
        name="mla_keys_values",
    )(u, u, norm_g.reshape(1, R), w_k, w_vt, tab)


def _attn_kernel(*refs, nseg, nsub):
    q_ref = refs[0]
    k_refs = refs[1:1 + nseg]
    vt_refs = refs[1 + nseg:1 + 2 * nseg]
    o_ref = refs[1 + 2 * nseg]
    rs = q_ref.shape[2] // nsub

    def scores(i):
        q = q_ref[0, 0, i * rs:(i + 1) * rs, :]
        return [_dot_nt(k[0, 0], q) for k in k_refs]

    def finish(i, s):
        m = functools.reduce(jnp.maximum, [jnp.max(x, axis=0, keepdims=True) for x in s])
        acc = functools.reduce(jnp.add, [_dot(vt[0, 0], jnp.exp2(x - m).astype(BF16)) for x, vt in zip(s, vt_refs)])
        o = acc[0:HEAD_DIM] / acc[HEAD_DIM:HEAD_DIM + 1]
        o_ref[i * rs:(i + 1) * rs, :] = o.T.astype(o_ref.dtype)

    s_next = scores(0)
    for i in range(nsub):
        s_cur = s_next
        if i + 1 < nsub:
            s_next = scores(i + 1)
        finish(i, s_cur)


def attention(q, ks, vts):
    B, Hm, Tq, _ = q.shape
    tq = _pick(Tq, (1024, 512, 256, 128))
    nq = Tq // tq
    nseg = len(ks)
    kspec = lambda a: pl.BlockSpec((1, 1) + a.shape[2:], lambda b, h, i: (b, h, 0, 0))
    return pl.pallas_call(
        functools.partial(_attn_kernel, nseg=nseg, nsub=max(1, tq // 256)),
        grid=(B, Hm, nq),
        in_specs=[pl.BlockSpec((1, 1, tq, QK_DIM), lambda b, h, i: (b, h, i, 0))]
                 + [kspec(a) for a in ks] + [kspec(a) for a in vts],
        out_specs=pl.BlockSpec((tq, HEAD_DIM), lambda b, h, i: (b * nq + i, h)),
        out_shape=jax.ShapeDtypeStruct((B * Tq, Hm * HEAD_DIM), BF16),
        compiler_params=_params("parallel", "parallel", "parallel"),
        name=f"attention_{nseg}seg",
    )(q, *ks, *vts)


def _merge_kernel(ac_ref, ag_ref, am_ref, wc_ref, wg_ref, wm_ref, gc_ref, gg_ref, gm_ref, o_ref):
    sig = lambda r: jax.nn.sigmoid(r[...].astype(F32))
    m = (sig(gc_ref) * _dot(ac_ref[...], wc_ref[0]) + sig(gg_ref) * _dot(ag_ref[...], wg_ref[0])
         + sig(gm_ref) * _dot(am_ref[...], wm_ref[0]))
    o_ref[...] = m.astype(o_ref.dtype)


def merge_branches(ac, ag, am, wc, wg, wm, l, u, off):
    N = ac.shape[0]
    D = wc.shape[2]
    tm = _pick(N, (1024, 512, 256, 128))
    tn = _pick((D, off["gate_conv"], off["gate_gdn"], off["gate_mla"]), (512, 256, 128))
    act = lambda a: pl.BlockSpec((tm, a.shape[1]), lambda i, j: (i, 0))
    wsp = lambda w: pl.BlockSpec((1, w.shape[1], tn), lambda i, j: (l, 0, j))
    gate = lambda name: pl.BlockSpec((tm, tn), lambda i, j: (i, off[name] // tn + j))
    return pl.pallas_call(
        _merge_kernel,
        grid=(N // tm, D // tn),
        in_specs=[act(ac), act(ag), act(am), wsp(wc), wsp(wg), wsp(wm),
                  gate("gate_conv"), gate("gate_gdn"), gate("gate_mla")],
        out_specs=pl.BlockSpec((tm, tn), lambda i, j: (i, j)),
        out_shape=jax.ShapeDtypeStruct((N, D), BF16),
        compiler_params=_params("parallel", "parallel"),
        name="merge_branches",
    )(ac, ag, am, wc, wg, wm, u, u, u)


def _out_router_kernel(m_ref, w_ref, h_ref, g1_ref, n2_ref, sh_ref, sc_ref, r_ref, o_ref, hn_ref, lg_ref):
    h_new = h_ref[...] + g1_ref[0] * _dot(m_ref[...], w_ref[0])
    o_ref[...] = h_new
    a = _norm_mod(h_new, n2_ref[...], sh_ref[0], sc_ref[0]).astype(BF16)
    hn_ref[...] = a
    lg_ref[0] = _dot_nt(r_ref[0], a)


def out_projection_router(m, w_o, l, h, gate, norm_g, sh, sc, router_t, B, T, rows_per_group):
    N, D = h.shape
    E = router_t.shape[1]
    tm = _pick(T, (512, 256, 128))
    nt = T // tm
    gdiv = rows_per_group // tm
    grp = pl.BlockSpec((1, 1, D), lambda i: (i // gdiv, 0, 0))
    return pl.pallas_call(
        _out_router_kernel,
        grid=(N // tm,),
        in_specs=[pl.BlockSpec((tm, m.shape[1]), lambda i: (i, 0)),
                  pl.BlockSpec((1, m.shape[1], D), lambda i: (l, 0, 0), pipeline_mode=pl.Buffered(1)),
                  pl.BlockSpec((tm, D), lambda i: (i, 0)),
                  grp, pl.BlockSpec((1, D), lambda i: (0, 0)), grp, grp,
                  pl.BlockSpec((1, E, D), lambda i: (l, 0, 0))],
        out_specs=[pl.BlockSpec((tm, D), lambda i: (i, 0)),
                   pl.BlockSpec((tm, D), lambda i: (i, 0)),
                   pl.BlockSpec((1, E, tm), lambda i: (i // nt, 0, i % nt))],
        out_shape=[jax.ShapeDtypeStruct((N, D), F32), jax.ShapeDtypeStruct((N, D), BF16),
                   jax.ShapeDtypeStruct((B, E, T), F32)],
        input_output_aliases={2: 0} if l > 0 else {},
        compiler_params=_params("parallel"),
        name="out_projection_router",
    )(m, w_o, h, gate, norm_g.reshape(1, D), sh, sc, router_t)


def _route_kernel(lg_ref, rm_ref, aff_ref, rmt_ref, *, T, E, cap, W):
    lg = lg_ref[0]
    mx = jnp.max(lg, axis=0, keepdims=True)
    ex = jnp.exp(lg - mx)
    aff = ex / jnp.sum(ex, axis=0, keepdims=True)
    aff_ref[0] = aff
    bits = lax.bitcast_convert_type(aff, jnp.int32)

    def count(mask):
        return jnp.sum(jnp.where(mask, 1.0, 0.0), axis=1, keepdims=True)

    thr = jnp.zeros((E, 1), jnp.int32)
    for bit in range(30, -1, -1):
        cand = thr | (1 << bit)
        thr = jnp.where(count(bits >= cand) >= cap, cand, thr)
    gt = bits > thr
    eq = bits == thr
    need = cap - count(gt)

    ri = lax.broadcasted_iota(jnp.int32, (W, W), 0)
    ci = lax.broadcasted_iota(jnp.int32, (W, W), 1)
    upper = jnp.where(ri < ci, 1.0, 0.0).astype(BF16)
    ident = jnp.where(ri == ci, 1.0, 0.0).astype(BF16)

    def excl_cumsum(mask):
        m = jnp.where(mask, 1.0, 0.0)
        outs, carry = [], jnp.zeros((E, 1), F32)
        for j in range(T // W):
            blk = m[:, j * W:(j + 1) * W]
            outs.append(_dot(blk.astype(BF16), upper) + carry)
            carry = carry + jnp.sum(blk, axis=1, keepdims=True)
        return jnp.concatenate(outs, axis=1) if len(outs) > 1 else outs[0]

    sel = gt | (eq & (excl_cumsum(eq) < need))
    rm = jnp.where(sel, excl_cumsum(sel), -1.0)
    rm_ref[0] = rm
    rmb = rm.astype(BF16)
    for j in range(T // W):
        rmt_ref[0, j * W:(j + 1) * W, :] = _dot_nt(ident, rmb[:, j * W:(j + 1) * W])


def route(logits_t, cap):
    B, E, T = logits_t.shape
    W = 256 if T % 256 == 0 else 128
    blk = pl.BlockSpec((1, E, T), lambda b: (b, 0, 0))
    return pl.pallas_call(
        functools.partial(_route_kernel, T=T, E=E, cap=cap, W=W),
        grid=(B,),
        in_specs=[blk],
        out_specs=[blk, blk, pl.BlockSpec((1, T, E), lambda b: (b, 0, 0))],
        out_shape=[jax.ShapeDtypeStruct((B, E, T), F32), jax.ShapeDtypeStruct((B, E, T), F32),
                   jax.ShapeDtypeStruct((B, T, E), F32)],
        compiler_params=_params("parallel"),
        name="route",
    )(logits_t)


def _gather_kernel(rm_ref, aff_ref, hn_ref, xs_ref, gate_ref, *, cap):
    rm = rm_ref[0, 0]
    T = rm.shape[1]
    slot = lax.broadcasted_iota(jnp.int32, (cap, T), 0).astype(F32)
    hit = slot == rm
    xs_ref[0, 0] = _dot(jnp.where(hit, 1.0, 0.0).astype(BF16), hn_ref[...]).astype(xs_ref.dtype)
    gate_ref[0, 0] = jnp.sum(jnp.where(hit, aff_ref[0, 0], 0.0), axis=1, keepdims=True)


def moe_gather(rm, aff, hn, cap):
    B, E, T = rm.shape
    D = hn.shape[1]
    rowspec = pl.BlockSpec((1, 1, 1, T), lambda b, e: (b, e, 0, 0))
    return pl.pallas_call(
        functools.partial(_gather_kernel, cap=cap),
        grid=(B, E),
        in_specs=[rowspec, rowspec, pl.BlockSpec((T, D), lambda b, e: (b, 0))],
        out_specs=[pl.BlockSpec((1, 1, cap, D), lambda b, e: (b, e, 0, 0)),
                   pl.BlockSpec((1, 1, cap, 1), lambda b, e: (b, e, 0, 0))],
        out_shape=[jax.ShapeDtypeStruct((B, E, cap, D), BF16), jax.ShapeDtypeStruct((B, E, cap, 1), F32)],
        compiler_params=_params("parallel", "parallel"),
        name="moe_gather",
    )(rm.reshape(B, E, 1, T), aff.reshape(B, E, 1, T), hn)


def _ffn_kernel(xs_ref, gate_ref, wg_ref, wu_ref, wd_ref, ye_ref):
    bb, _, cap, D = xs_ref.shape
    xs = xs_ref[...].reshape(bb * cap, D)
    hid = (_silu(_dot(xs, wg_ref[0, 0])) * _dot(xs, wu_ref[0, 0])).astype(BF16)
    ye = _dot(hid, wd_ref[0, 0]) * gate_ref[...].reshape(bb * cap, 1)
    ye_ref[...] = ye.reshape(bb, 1, cap, D).astype(ye_ref.dtype)


def moe_ffn(xs, gate, w_gate, w_up, w_down, l):
    B, E, cap, D = xs.shape
    FF = w_gate.shape[3]
    bb = max(1, min(B, 512 // cap))
    while B % bb:
        bb -= 1
    tok = lambda last: pl.BlockSpec((bb, 1, cap, last), lambda e, b: (b, e, 0, 0))
    return pl.pallas_call(
        _ffn_kernel,
        grid=(E, B // bb),
        in_specs=[tok(D), tok(1),
                  pl.BlockSpec((1, 1, D, FF), lambda e, b: (l, e, 0, 0)),
                  pl.BlockSpec((1, 1, D, FF), lambda e, b: (l, e, 0, 0)),
                  pl.BlockSpec((1, 1, FF, D), lambda e, b: (l, e, 0, 0))],
        out_specs=tok(D),
        out_shape=jax.ShapeDtypeStruct((B, E, cap, D), BF16),
        compiler_params=_params("parallel", "parallel"),
        name="moe_ffn",
    )(xs, gate, w_gate, w_up, w_down)


def _scatter_kernel(rmt_ref, ye_ref, h_ref, g_ref, o_ref, *, E, cap):
    tt = rmt_ref.shape[1]
    rmt = rmt_ref[0]
    slot = lax.broadcasted_iota(jnp.int32, (tt, cap), 1).astype(F32)
    acc = jnp.zeros(o_ref.shape, F32)
    for e in range(E):
        onehot = jnp.where(rmt[:, e:e + 1] == slot, 1.0, 0.0).astype(BF16)
        acc = acc + _dot(onehot, ye_ref[0, e * cap:(e + 1) * cap, :])
    o_ref[...] = h_ref[...] + g_ref[0] * acc


def moe_scatter(rmt, ye, h, gate, T):
    B, _, E = rmt.shape
    _, EC, D = ye.shape
    cap = EC // E
    tt = _pick(T, (512, 256, 128))
    tn = _pick(D, (1024, 512, 256, 128))
    nt = T // tt
    return pl.pallas_call(
        functools.partial(_scatter_kernel, E=E, cap=cap),
        grid=(B, D // tn, nt),
        in_specs=[pl.BlockSpec((1, tt, E), lambda b, j, i: (b, i, 0)),
                  pl.BlockSpec((1, EC, tn), lambda b, j, i: (b, 0, j)),
                  pl.BlockSpec((tt, tn), lambda b, j, i: (b * nt + i, j)),
                  pl.BlockSpec((1, 1, tn), lambda b, j, i: (b, 0, j))],
        out_specs=pl.BlockSpec((tt, tn), lambda b, j, i: (b * nt + i, j)),
        out_shape=jax.ShapeDtypeStruct(h.shape, F32),
        input_output_aliases={2: 0},
        compiler_params=_params("parallel", "parallel", "parallel"),
        name="moe_scatter",
    )(rmt, ye, h, gate)


def expert_choice_moe(h, hn, logits_t, gate, w_gate, w_up, w_down, l, B, T):
    E = logits_t.shape[1]
    cap = EC_CAPACITY * T // E
    assert cap <= 256 and cap % 16 == 0, "slot indices must stay exact in bf16 and fill bf16 sublane tiles"
    rm, aff, rmt = route(logits_t, cap)
    xs, gsl = moe_gather(rm, aff, hn, cap)
    ye = moe_ffn(xs, gsl, w_gate, w_up, w_down, l)
    return moe_scatter(rmt, ye.reshape(B, E * cap, ye.shape[-1]), h, gate, T)


def _final_norm_kernel(x_ref, g_ref, o_ref):
    o_ref[...] = _rms(x_ref[...], g_ref[...])


def final_norm(x, g):
    N, D = x.shape
    tm = _pick(N, (512, 256, 128))
    return pl.pallas_call(
        _final_norm_kernel,
        grid=(N // tm,),
        in_specs=[pl.BlockSpec((tm, D), lambda i: (i, 0)), pl.BlockSpec((1, D), lambda i: (0, 0))],
        out_specs=pl.BlockSpec((tm, D), lambda i: (i, 0)),
        out_shape=jax.ShapeDtypeStruct((N, D), F32),
        compiler_params=_params("parallel"),
        name="final_norm",
    )(x, g.reshape(1, D))


def _swap_rope(w):
    half = ROPE_DIM // 2
    return jnp.concatenate([-w[..., half:], w[..., :half]], axis=-1)


_PACKED_ORDER = ("mla_cq", "gdn_small", "mla_kpe", "mla_ckv", "conv_x", "conv_b", "conv_c", "gdn_q", "gdn_k", "gdn_v",
                 "gdn_z", "gate_conv", "gate_gdn", "gate_mla")


def _in_layout(C, Hg, RQ, RKV, D):
    width = {"mla_cq": RQ, "gdn_small": LANE, "mla_kpe": LANE, "mla_ckv": RKV, "conv_x": C, "conv_b": C, "conv_c": C,
             "gdn_q": Hg * HEAD_DIM, "gdn_k": Hg * HEAD_DIM, "gdn_v": Hg * HEAD_DIM, "gdn_z": Hg * HEAD_DIM,
             "gate_conv": D, "gate_gdn": D, "gate_mla": D}
    off, o = {}, 0
    for name in _PACKED_ORDER:
        off[name] = o
        o += width[name]
    assert off["mla_cq"] % RQ == 0 and off["mla_ckv"] % RKV == 0
    return off, o


def _pack_w_in(w, C, Hg, RQ, RKV, D):
    names = ("conv_x", "conv_b", "conv_c", "gdn_q", "gdn_k", "gdn_v", "gdn_z", "gdn_small", "mla_cq", "mla_ckv",
             "mla_kpe", "gate_conv", "gate_gdn", "gate_mla")
    sizes = (C, C, C, Hg * HEAD_DIM, Hg * HEAD_DIM, Hg * HEAD_DIM, Hg * HEAD_DIM, 4 * Hg, RQ, RKV, ROPE_DIM, D, D, D)
    w = w.astype(BF16)
    seg, o = {}, 0
    for name, size in zip(names, sizes):
        seg[name] = w[..., o:o + size]
        o += size
    seg["gdn_small"] = jnp.pad(seg["gdn_small"], ((0, 0), (0, 0), (0, LANE - 4 * Hg)))
    seg["mla_kpe"] = jnp.concatenate([seg["mla_kpe"], _swap_rope(seg["mla_kpe"])], axis=-1)
    return jnp.concatenate([seg[n] for n in _PACKED_ORDER], axis=-1)


def _rope_tables(T):
    rows = T // GRID_W
    row = jnp.repeat(jnp.arange(rows), GRID_W)
    col = jnp.tile(jnp.arange(GRID_W), rows)
    n_freq = ROPE_DIM // 4
    inv_freq = 1.0 / (ROPE_THETA ** (jnp.arange(n_freq, dtype=F32) / n_freq))
    ang = jnp.concatenate([row[:, None] * inv_freq, col[:, None] * inv_freq], axis=-1)
    cos, sin = jnp.cos(ang), jnp.sin(ang)
    return jnp.concatenate([cos, cos], axis=1), jnp.concatenate([sin, sin], axis=1)


def kernel(x, c, ctx, c_ctx, w_ada, b_ada, norm1_g, w_in, conv_w, conv_out, gdn_conv_w, gdn_a_log, gdn_dt_bias,
           gdn_norm_g, gdn_out, mla_q_norm_g, mla_w_uq, mla_kv_norm_g, mla_w_ukv, mla_out, w_o, norm2_g, router_w,
           w_gate, w_up, w_down, final_g):
    B, T, D = x.shape
    TC = ctx.shape[1]
    L = w_ada.shape[0]
    C = conv_w.shape[-1]
    Hg = gdn_a_log.shape[-1]
    RQ = mla_q_norm_g.shape[-1]
    RKV = mla_kv_norm_g.shape[-1]
    Hm = mla_w_uq.shape[-1] // QK_DIM
    assert 4 * Hg <= LANE and T % GDN_CHUNK == 0 and TC % GDN_CHUNK == 0

    off, _ = _in_layout(C, Hg, RQ, RKV, D)

    R = -(-(B + 1) // 8) * 8
    cc = jnp.concatenate([c, c_ctx[None, :], jnp.zeros((R - B - 1, D), F32)], axis=0)
    mod = ada_modulation(cc, w_ada, b_ada)

    cos64, sin64 = _rope_tables(T)
    q_ct, q_st = jnp.tile(cos64, (1, Hm)), jnp.tile(sin64, (1, Hm))
    q_ct0, q_st0 = jnp.ones((TC, Hm * ROPE_DIM), F32), jnp.zeros((TC, Hm * ROPE_DIM), F32)
    k_tab = jnp.concatenate([cos64, sin64], 1)
    k_tab0 = jnp.concatenate([jnp.ones((TC, ROPE_DIM), F32), jnp.zeros((TC, ROPE_DIM), F32)], 1)

    h = x.reshape(B * T, D)
    hc = ctx.reshape(B * TC, D)
    zero_state = jnp.zeros((B, Hg, HEAD_DIM, HEAD_DIM), F32)

    w_in_p = _pack_w_in(w_in, C, Hg, RQ, RKV, D)
    wc_o, wg_o, wm_o, wo_b = (a.astype(BF16) for a in (conv_out, gdn_out, mla_out, w_o))
    router_t = jnp.swapaxes(router_w, 1, 2).astype(BF16)
    wgt, wup, wdn = w_gate.astype(BF16), w_up.astype(BF16), w_down.astype(BF16)

    for l in range(L):
        need_ctx = l < L - 1
        wq = mla_w_uq[l].reshape(RQ, Hm, QK_DIM)
        wq_pe = wq[..., HEAD_DIM:]
        wq_main = jnp.concatenate([wq[..., :HEAD_DIM].reshape(RQ, -1), wq_pe.reshape(RQ, -1)], 1).astype(BF16)
        wq_aux = _swap_rope(wq_pe).reshape(RQ, -1).astype(BF16)
        wkv = mla_w_ukv[l].reshape(RKV, Hm, 2 * HEAD_DIM)
        w_k = wkv[..., :HEAD_DIM].reshape(RKV, -1).astype(BF16)
        w_vt = wkv[..., HEAD_DIM:].transpose(1, 2, 0).astype(BF16)
        pad = jnp.zeros((LANE - 2 * Hg,), F32)
        alog_row = jnp.concatenate([gdn_a_log[l].reshape(-1), pad]).reshape(1, LANE)
        dtb_row = jnp.concatenate([gdn_dt_bias[l].reshape(-1), pad]).reshape(1, LANE)

        ml = mod[l, :B].reshape(B, 1, 6 * D)
        mc = mod[l, B:B + 1].reshape(1, 1, 6 * D)
        sh1, sc1, g1, sh2, sc2, g2 = (ml[..., i * D:(i + 1) * D] for i in range(6))
        csh1, csc1, cg1, csh2, csc2, cg2 = (mc[..., i * D:(i + 1) * D] for i in range(6))

        u_lat = in_projection(norm_modulate(h, norm1_g[l], sh1, sc1, T), w_in_p, l)
        u_ctx = in_projection(norm_modulate(hc, norm1_g[l], csh1, csc1, B * TC), w_in_p, l)

        ag_ctx, s_f, s_b = gdn_mixer(u_ctx, gdn_conv_w[l], alog_row, dtb_row, gdn_norm_g[l], zero_state, zero_state,
                                     B, TC, Hg, off)
        ag_lat, _, _ = gdn_mixer(u_lat, gdn_conv_w[l], alog_row, dtb_row, gdn_norm_g[l], s_f, s_b, B, T, Hg, off)

        k_ctx, v_ctx = mla_keys_values(u_ctx, mla_kv_norm_g[l], w_k, w_vt, k_tab0, B, TC, Hm, off)
        k_lat, v_lat = mla_keys_values(u_lat, mla_kv_norm_g[l], w_k, w_vt, k_tab, B, T, Hm, off)
        q_lat = mla_queries(u_lat, mla_q_norm_g[l], wq_main, wq_aux, q_ct, q_st, B, T, Hm, off)
        am_lat = attention(q_lat, [k_ctx, k_lat], [v_ctx, v_lat])

        ac_lat = conv_mixer_front(u_lat, conv_w[l], B, T, off)
        m_lat = merge_branches(ac_lat, ag_lat, am_lat, wc_o, wg_o, wm_o, l, u_lat, off)
        h, hn, lg = out_projection_router(m_lat, wo_b, l, h, g1, norm2_g[l], sh2, sc2, router_t, B, T, T)
        h = expert_choice_moe(h, hn, lg, g2, wgt, wup, wdn, l, B, T)

        if need_ctx:
            q_ctx = mla_queries(u_ctx, mla_q_norm_g[l], wq_main, wq_aux, q_ct0, q_st0, B, TC, Hm, off)
            am_ctx = attention(q_ctx, [k_ctx], [v_ctx])
            ac_ctx = conv_mixer_front(u_ctx, conv_w[l], B, TC, off)
            m_ctx = merge_branches(ac_ctx, ag_ctx, am_ctx, wc_o, wg_o, wm_o, l, u_ctx, off)
            hc, hn, lg = out_projection_router(m_ctx, wo_b, l, hc, cg1, norm2_g[l], csh2, csc2, router_t, B, TC,
                                               B * TC)
            hc = expert_choice_moe(hc, hn, lg, jnp.broadcast_to(cg2, (B, 1, D)), wgt, wup, wdn, l, B, TC)

    return final_norm(h, final_g).reshape(B, T, D)
```

```python
import functools
import math

import jax
import jax.numpy as jnp
from jax import lax
from jax.experimental import pallas as pl
from jax.experimental.pallas import tpu as pltpu

F32 = jnp.float32
BF16 = jnp.bfloat16

RMS_EPS = 1e-6
L2_EPS = 1e-6
ROPE_THETA = 10000.0
GRID_W = 64
HEAD_DIM = 128
ROPE_DIM = 64
QK_DIM = HEAD_DIM + ROPE_DIM
GDN_CHUNK = 128
GDN_SUB = 16
EC_CAPACITY = 2
LANE = 128
V_PAD = 16
ATTN_LOGIT_SCALE =QK_DIM ** -0.5 * math.log2(math.e)
VMEM_LIMIT = 56 * 1024 * 1024


def _pick(n, cands):
    ns = n if isinstance(n, tuple) else (n,)
    for c in cands:
        if all(v % c == 0 for v in ns):
            return c
    raise ValueError(f"no tile in {cands} divides {ns}")


def _params(*sem):
    return pltpu.CompilerParams(dimension_semantics=sem, vmem_limit_bytes=VMEM_LIMIT)


def _dot(a, b):
    return jnp.dot(a, b, preferred_element_type=F32)


def _dot_nt(a, b):
    return lax.dot_general(a, b, (((1,), (1,)), ((), ())), preferred_element_type=F32)


def _dot_tn(a, b):
    return lax.dot_general(a, b, (((0,), (0,)), ((), ())), preferred_element_type=F32)


def _split2(x):
    hi = x.astype(BF16)
    lo = (x - hi.astype(F32)).astype(BF16)
    return hi, lo


def _split3(x):
    hi = x.astype(BF16)
    r = x - hi.astype(F32)
    mid = r.astype(BF16)
    lo = (r - mid.astype(F32)).astype(BF16)
    return hi, mid, lo


def _silu(x):
    return x * jax.nn.sigmoid(x)


def _softplus(x):
    return jnp.maximum(x, 0.0) + jnp.log(1.0 + jnp.exp(-jnp.abs(x)))


def _ada_kernel(c_ref, w_ref, b_ref, o_ref):
    a = _silu(c_ref[...])
    ah, al = _split2(a)
    wh, wl = _split2(w_ref[0])
    o_ref[0] = _dot(ah, wh) + (_dot(ah, wl) + _dot(al, wh)) + b_ref[0]


def ada_modulation(cc, w_ada, b_ada):
    L, D, D6 = w_ada.shape
    R = cc.shape[0]
    tn = _pick(D6, (1024, 512, 256, 128))
    return pl.pallas_call(
        _ada_kernel,
        grid=(L, D6 // tn),
        in_specs=[pl.BlockSpec((R, D), lambda l, j: (0, 0)),
                  pl.BlockSpec((1, D, tn), lambda l, j: (l, 0, j)),
                  pl.BlockSpec((1, 1, tn), lambda l, j: (l, 0, j))],
        out_specs=pl.BlockSpec((1, R, tn), lambda l, j: (l, 0, j)),
        out_shape=jax.ShapeDtypeStruct((L, R, D6), F32),
        compiler_params=_params("parallel", "parallel"),
        name="ada_modulation",
    )(cc, w_ada, b_ada.reshape(L, 1, D6))


def _norm_mod(x, g, sh, sc):
    ms = jnp.mean(x * x, axis=-1, keepdims=True)
    y = x * lax.rsqrt(ms + RMS_EPS) * g
    return y * (1.0 + sc) + sh


def _norm_mod_kernel(x_ref, g_ref, sh_ref, sc_ref, o_ref):
    o_ref[...] = _norm_mod(x_ref[...], g_ref[...], sh_ref[0], sc_ref[0]).astype(o_ref.dtype)


def norm_modulate(x, g, sh, sc, rows_per_group):
    N, D = x.shape
    tm = _pick(rows_per_group, (512, 256, 128))
    gdiv = rows_per_group // tm
    return pl.pallas_call(
        _norm_mod_kernel,
        grid=(N // tm,),
        in_specs=[pl.BlockSpec((tm, D), lambda i: (i, 0)),
                  pl.BlockSpec((1, D), lambda i: (0, 0)),
                  pl.BlockSpec((1, 1, D), lambda i: (i // gdiv, 0, 0)),
                  pl.BlockSpec((1, 1, D), lambda i: (i // gdiv, 0, 0))],
        out_specs=pl.BlockSpec((tm, D), lambda i: (i, 0)),
        out_shape=jax.ShapeDtypeStruct((N, D), BF16),
        compiler_params=_params("parallel"),
        name="norm_modulate",
    )(x, g.reshape(1, D), sh, sc)


def _in_proj_kernel(a_ref, w_ref, o_ref):
    o_ref[...] = _dot(a_ref[...], w_ref[0]).astype(o_ref.dtype)


def in_projection(a, w, l):
    N, D = a.shape
    NC = w.shape[2]
    tm = _pick(N, (2048, 1024, 512, 256, 128))
    tn = _pick(NC, (512, 256, 128))
    return pl.pallas_call(
        _in_proj_kernel,
        grid=(N // tm, NC // tn),
        in_specs=[pl.BlockSpec((tm, D), lambda i, j: (i, 0)),
                  pl.BlockSpec((1, D, tn), lambda i, j: (l, 0, j))],
        out_specs=pl.BlockSpec((tm, tn), lambda i, j: (i, j)),
        out_shape=jax.ShapeDtypeStruct((N, NC), BF16),
        compiler_params=_params("parallel", "parallel"),
        name="in_projection",
    )(a, w)


def _conv3(x, w, T):
    row = lax.broadcasted_iota(jnp.int32, x.shape, 0)
    prev = jnp.where(row == 0, 0.0, pltpu.roll(x, 1, 0))
    nxt = jnp.where(row == T - 1, 0.0, pltpu.roll(x, T - 1, 0))
    return prev * w[0:1] + x * w[1:2] + nxt * w[2:3]


def _conv_mixer_kernel(x_ref, b_ref, c_ref, w_ref, o_ref, *, T):
    v = c_ref[...].astype(F32) * x_ref[...].astype(F32)
    o_ref[...] = (b_ref[...].astype(F32) * _conv3(v, w_ref[...], T)).astype(o_ref.dtype)


def conv_mixer_front(u, conv_w, B, T, off):
    C = conv_w.shape[1]
    tc = _pick(C, (256, 128))
    ox, ob, oc = (off[k] // tc for k in ("conv_x", "conv_b", "conv_c"))
    return pl.pallas_call(
        functools.partial(_conv_mixer_kernel, T=T),
        grid=(B, C // tc),
        in_specs=[pl.BlockSpec((T, tc), lambda b, j: (b, ox + j)),
                  pl.BlockSpec((T, tc), lambda b, j: (b, ob + j)),
                  pl.BlockSpec((T, tc), lambda b, j: (b, oc + j)),
                  pl.BlockSpec((3, tc), lambda b, j: (0, j))],
        out_specs=pl.BlockSpec((T, tc), lambda b, j: (b, j)),
        out_shape=jax.ShapeDtypeStruct((B * T, C), BF16),
        compiler_params=_params("parallel", "parallel"),
        name="conv_mixer_front",
    )(u, u, u, conv_w)


def _gdn_solve(a_list, rhs_list, sub_mask, eye):
    g = a_list[0].shape[0]
    b16 = lambda ts: [t.astype(BF16) for t in ts]
    dots = lambda xs, ys: [_dot(x, y) for x, y in zip(xs, ys)]
    d = [jnp.where(sub_mask, a, 0.0) for a in a_list]
    n = [a - dd for a, dd in zip(a_list, d)]
    db = b16(d)
    d2b = b16(dots(db, db))
    d4b = b16(dots(d2b, d2b))
    d8b = b16(dots(d4b, d4b))
    x = [eye - dd for dd in d]
    for p in (d2b, d4b, d8b):
        x = [xx + t for xx, t in zip(x, dots(b16(x), p))]
    z = dots(b16(x), b16([jnp.concatenate([nn, r], axis=1) for nn, r in zip(n, rhs_list)]))
    mb = b16([zz[:, :g] for zz in z])
    zr = [zz[:, g:] for zz in z]
    powers = [mb]
    for _ in range(int(math.log2(GDN_CHUNK // GDN_SUB)) - 1):
        powers.append(b16(dots(powers[-1], powers[-1])))
    for p in reversed(powers[1:]):
        zr = [r + t for r, t in zip(zr, dots(p, b16(zr)))]
    return [r - t for r, t in zip(zr, dots(mb, b16(zr)))]


def _gdn_kernel(q_ref, k_ref, v_ref, z_ref, ab_ref, wq_ref, wk_ref, wv_ref, alog_ref, dtb_ref, ng_ref,
                s0f_ref, s0b_ref, o_ref, sf_ref, sb_ref,
                q_s, k_s, v_s, col_s, u_s, w_s, qg_s, kd_s, in_s, eg_s, o_s, st_s, *, T, H, G, GI, HB):
    hb = pl.program_id(1)
    NC = T // GDN_CHUNK
    L = GDN_CHUNK

    def l2n(x):
        return x * lax.rsqrt(jnp.sum(x * x, axis=-1, keepdims=True) + L2_EPS)

    ri = lax.broadcasted_iota(jnp.int32, (G, G), 0)
    ci = lax.broadcasted_iota(jnp.int32, (G, G), 1)
    same_chunk = (ri >> int(math.log2(L))) == (ci >> int(math.log2(L)))
    same_sub = (ri >> int(math.log2(GDN_SUB))) == (ci >> int(math.log2(GDN_SUB)))
    eye = jnp.where(ri == ci, 1.0, 0.0).astype(F32)
    ones_chunk = jnp.where(same_chunk, 1.0, 0.0).astype(BF16)

    ab = ab_ref[...].astype(F32)
    lane = lax.broadcasted_iota(jnp.int32, ab.shape, 1)
    gfull = -jnp.exp(alog_ref[...]) * _softplus(ab + dtb_ref[...])
    bfull = jax.nn.sigmoid(ab)

    def col(x, idx):
        return jnp.sum(jnp.where(lane == idx, x, 0.0), axis=1, keepdims=True)

    for hh in range(HB):
        hsl = slice(hh * HEAD_DIM, (hh + 1) * HEAD_DIM)
        head = hb * HB + hh

        def prep(x_ref, w_ref):
            return _silu(_conv3(x_ref[:, hsl].astype(F32), w_ref[:, hsl], T))

        q_s[hh] = l2n(prep(q_ref, wq_ref)) * (HEAD_DIM ** -0.5)
        k_s[hh] = l2n(prep(k_ref, wk_ref))
        v_s[hh] = prep(v_ref, wv_ref)

        g_f, g_b = col(gfull, head), col(gfull, H + head)
        b_f, b_b = col(bfull, 2 * H + head), col(bfull, 3 * H + head)
        col_s[hh] = jnp.where(lane == 0, g_f, jnp.where(lane == 1, g_b, jnp.where(lane == 2, b_f, b_b)))
        st_s[hh, 0] = s0f_ref[0, hh]
        st_s[hh, 1] = s0b_ref[0, hh]

    chains = [(hh, d) for hh in range(HB) for d in range(2)]
    incl = [same_chunk & (ci <= ri), same_chunk & (ci >= ri)]
    strict = [same_chunk & (ci < ri), same_chunk & (ci > ri)]
    incl_b = [jnp.where(m, 1.0, 0.0).astype(BF16) for m in incl]
    ones_8g = jnp.ones((8, G), BF16)

    def exact3_list(ms, xs):
        parts = [_split3(x) for x in xs]
        return [_dot(m, p[0]) + _dot(m, p[1]) + _dot(m, p[2]) for m, p in zip(ms, parts)]

    def group_body(gi, carry):
        units = [(j, hh) for j in range(GI) for hh in range(HB)]
        gch = [(ui, d) for ui in range(len(units)) for d in range(2)]
        r0 = [pl.multiple_of((gi * GI + j) * G, G) for j in range(GI)]
        rows = [pl.ds(r0[j], G) for j, _ in units]
        qg = [q_s[hh, r, :] for (_, hh), r in zip(units, rows)]
        kg = [k_s[hh, r, :] for (_, hh), r in zip(units, rows)]
        vg = [v_s[hh, r, :] for (_, hh), r in zip(units, rows)]
        cols = [col_s[hh, r, :] for (_, hh), r in zip(units, rows)]
        kb16 = [k.astype(BF16) for k in kg]
        kk = [_dot_nt(kb, kb) for kb in kb16]
        qk = [_dot_nt(q.astype(BF16), kb) for q, kb in zip(qg, kb16)]
        bcol = [cols[ui][:, 2 + d:3 + d] for ui, d in gch]
        gb = [jnp.broadcast_to(cols[ui][:, d:d + 1], (G, LANE)) for ui, d in gch]
        gc = exact3_list([incl_b[d] for _, d in gch], gb)
        gl = exact3_list([ones_chunk] * len(gch), gb)
        cmat = [jnp.concatenate([g] * (G // LANE), axis=1) if G > LANE else g for g in gc]
        rrow = [r[0:1] for r in exact3_list([ones_8g] * len(gch),
                                            [jnp.where(ri == ci, cm, 0.0) for cm in cmat])]
        decay = [jnp.where(incl[d], jnp.exp(jnp.where(incl[d], cm - rr, 0.0)), 0.0)
                 for (_, d), cm, rr in zip(gch, cmat, rrow)]
        a = [jnp.where(strict[d], bc * kk[ui] * dc, 0.0) for (ui, d), bc, dc in zip(gch, bcol, decay)]
        egc = [jnp.exp(g) for g in gc]
        rhs = [jnp.concatenate([vg[ui] * bc, kg[ui] * bc * e], axis=1) for (ui, _), bc, e in zip(gch, bcol, egc)]
        uw = _gdn_solve(a, rhs, same_sub, eye)
        for i, (ui, d) in enumerate(gch):
            j, hh = units[ui]
            u_s[hh, d, rows[ui], :] = uw[i][:, :HEAD_DIM]
            w_s[hh, d, rows[ui], :] = uw[i][:, HEAD_DIM:].astype(BF16)
            qg_s[hh, d, rows[ui], :] = (qg[ui] * egc[i]).astype(BF16)
            kd_s[hh, d, rows[ui], :] = (kg[ui] * jnp.exp(gl[i] - gc[i])).astype(BF16)
            egl = jnp.exp(gl[i])
            intra = (qk[ui] * decay[i]).astype(BF16)
            for c in range(G // L):
                e0 = pl.multiple_of(((gi * GI + j) * (G // L) + c) * 8, 8)
                eg_s[hh, d, pl.ds(e0, 8), :] = egl[c * L:c * L + 8]
                in_s[hh, d, pl.ds(r0[j] + c * L, L), :] = intra[c * L:(c + 1) * L, c * L:(c + 1) * L]
        return carry

    lax.fori_loop(0, T // (G * GI), group_body, 0)

    def scan_body(c, carry):
        rws = [pl.ds(pl.multiple_of((c if d == 0 else NC - 1 - c) * L, L), L) for _, d in chains]
        egr = [pl.ds(pl.multiple_of((c if d == 0 else NC - 1 - c) * 8, 8), 8) for _, d in chains]
        s = [st_s[hh, d] for hh, d in chains]
        wq = [jnp.concatenate([w_s[hh, d, r, :], qg_s[hh, d, r, :]], axis=0) for (hh, d), r in zip(chains, rws)]
        ws = [_dot(x, y.astype(BF16)) for x, y in zip(wq, s)]
        vb = [(u_s[hh, d, r, :] - w[:L]).astype(BF16) for (hh, d), r, w in zip(chains, rws, ws)]
        oi = [_dot(in_s[hh, d, r, :], v) for (hh, d), r, v in zip(chains, rws, vb)]
        ds_ = [_dot_tn(kd_s[hh, d, r, :], v) for (hh, d), r, v in zip(chains, rws, vb)]
        for i, (hh, d) in enumerate(chains):
            o_s[hh, d, rws[i], :] = ws[i][L:] + oi[i]
            st_s[hh, d] = s[i] * eg_s[hh, d, egr[i], :][0:1] + ds_[i]
        return carry

    lax.fori_loop(0, NC, scan_body, 0)

    for hh in range(HB):
        hsl = slice(hh * HEAD_DIM, (hh + 1) * HEAD_DIM)
        sf_ref[0, hh] = st_s[hh, 0]
        sb_ref[0, hh] = st_s[hh, 1]
        o = o_s[hh, 0] + o_s[hh, 1]
        y = o * lax.rsqrt(jnp.mean(o * o, axis=-1, keepdims=True) + RMS_EPS) * ng_ref[...]
        o_ref[:, hsl] = (y * _silu(z_ref[:, hsl].astype(F32))).astype(o_ref.dtype)


def gdn_mixer(u, conv_w, alog_row, dtb_row, norm_g, s0f, s0b, B, T, H, off):
    G = GDN_CHUNK
    assert G == LANE
    GI = 2 if T % (2 * G) == 0 else 1
    HB = 2 if H % 2 == 0 else 1
    W = HB * HEAD_DIM
    assert all(off[k] % W == 0 for k in ("gdn_q", "gdn_k", "gdn_v", "gdn_z"))
    oq, ok, ov, oz = (off[k] // W for k in ("gdn_q", "gdn_k", "gdn_v", "gdn_z"))
    oab = off["gdn_small"] // LANE
    seq = lambda o: pl.BlockSpec((T, W), lambda b, h: (b, o + h))
    cw = lambda o: pl.BlockSpec((3, W), lambda b, h: (0, o * (H // HB) + h))
    row = pl.BlockSpec((1, LANE), lambda b, h: (0, 0))
    st = pl.BlockSpec((1, HB, HEAD_DIM, HEAD_DIM), lambda b, h: (b, h, 0, 0))
    tbuf = lambda: pltpu.VMEM((HB, T, HEAD_DIM), F32)
    hbuf = lambda dt: pltpu.VMEM((HB, 2, T, HEAD_DIM), dt)
    return pl.pallas_call(
        functools.partial(_gdn_kernel, T=T, H=H, G=G, GI=GI, HB=HB),
        grid=(B, H // HB),
        in_specs=[seq(oq), seq(ok), seq(ov), seq(oz),
                  pl.BlockSpec((T, LANE), lambda b, h: (b, oab)),
                  cw(0), cw(1), cw(2), row, row, row, st, st],
        out_specs=[pl.BlockSpec((T, W), lambda b, h: (b, h)), st, st],
        out_shape=[jax.ShapeDtypeStruct((B * T, H * HEAD_DIM), BF16),
                   jax.ShapeDtypeStruct((B, H, HEAD_DIM, HEAD_DIM), F32),
                   jax.ShapeDtypeStruct((B, H, HEAD_DIM, HEAD_DIM), F32)],
        scratch_shapes=[tbuf(), tbuf(), tbuf(), tbuf(), hbuf(F32), hbuf(BF16), hbuf(BF16), hbuf(BF16),
                        pltpu.VMEM((HB, 2, T, GDN_CHUNK), BF16), pltpu.VMEM((HB, 2, T // 8, HEAD_DIM), F32),
                        hbuf(F32), pltpu.VMEM((HB, 2, HEAD_DIM, HEAD_DIM), F32)],
        compiler_params=_params("parallel", "parallel"),
        name="gdn_mixer",
    )(u, u, u, u, u, conv_w, conv_w, conv_w, alog_row, dtb_row, norm_g.reshape(1, HEAD_DIM), s0f, s0b)


def _rms(x, g):
    return x * lax.rsqrt(jnp.mean(x * x, axis=-1, keepdims=True) + RMS_EPS) * g


def _mla_q_kernel(x_ref, g_ref, wm_ref, wa_ref, ct_ref, st_ref, o_ref, *, Hm):
    a = _rms(x_ref[...].astype(F32), g_ref[...]).astype(BF16)
    main = _dot(a, wm_ref[...]) * ATTN_LOGIT_SCALE
    nn = Hm * HEAD_DIM
    pe = main[:, nn:] * ct_ref[...] + (_dot(a, wa_ref[...]) * ATTN_LOGIT_SCALE) * st_ref[...]
    for h in range(Hm):
        o_ref[0, h, :, 0:HEAD_DIM] = main[:, h * HEAD_DIM:(h + 1) * HEAD_DIM].astype(o_ref.dtype)
        o_ref[0, h, :, HEAD_DIM:QK_DIM] = pe[:, h * ROPE_DIM:(h + 1) * ROPE_DIM].astype(o_ref.dtype)


def mla_queries(u, norm_g, w_main, w_aux, ctab, stab, B, T, Hm, off):
    R = w_main.shape[0]
    tm = _pick(T, (512, 256, 128))
    nt = T // tm
    oc = off["mla_cq"] // R
    full = lambda a: pl.BlockSpec(a.shape, lambda i: (0, 0))
    tab = pl.BlockSpec((tm, Hm * ROPE_DIM), lambda i: (i % nt, 0))
    return pl.pallas_call(
        functools.partial(_mla_q_kernel, Hm=Hm),
        grid=(B * nt,),
        in_specs=[pl.BlockSpec((tm, R), lambda i: (i, oc)), pl.BlockSpec((1, R), lambda i: (0, 0)),
                  full(w_main), full(w_aux), tab, tab],
        out_specs=pl.BlockSpec((1, Hm, tm, QK_DIM), lambda i: (i // nt, 0, i % nt, 0)),
        out_shape=jax.ShapeDtypeStruct((B, Hm, T, QK_DIM), BF16),
        compiler_params=_params("parallel"),
        name="mla_queries",
    )(u, norm_g.reshape(1, R), w_main, w_aux, ctab, stab)


def _mla_kv_kernel(x_ref, pe_ref, g_ref, wk_ref, wvt_ref, tab_ref, k_ref, vt_ref, *, Hm):
    a = _rms(x_ref[...].astype(F32), g_ref[...]).astype(BF16)
    kn = _dot(a, wk_ref[...])
    r = pe_ref[...].astype(F32) * tab_ref[...]
    pe = (r + pltpu.roll(r, ROPE_DIM, 1))[:, 0:ROPE_DIM].astype(k_ref.dtype)
    for h in range(Hm):
        k_ref[0, h, :, 0:HEAD_DIM] = kn[:, h * HEAD_DIM:(h + 1) * HEAD_DIM].astype(k_ref.dtype)
        k_ref[0, h, :, HEAD_DIM:QK_DIM] = pe
        vt_ref[0, h, 0:HEAD_DIM, :] = _dot_nt(wvt_ref[h], a).astype(vt_ref.dtype)
        vt_ref[0, h, HEAD_DIM:, :] = jnp.ones((V_PAD, a.shape[0]), vt_ref.dtype)


def mla_keys_values(u, norm_g, w_k, w_vt, tab, B, T, Hm, off):
    R = w_k.shape[0]
    tm = _pick(T, (512, 256, 128))
    nt = T // tm
    oc, op = off["mla_ckv"] // R, off["mla_kpe"] // LANE
    return pl.pallas_call(
        functools.partial(_mla_kv_kernel, Hm=Hm),
        grid=(B * nt,),
        in_specs=[pl.BlockSpec((tm, R), lambda i: (i, oc)),
                  pl.BlockSpec((tm, LANE), lambda i: (i, op)),
                  pl.BlockSpec((1, R), lambda i: (0, 0)),
                  pl.BlockSpec(w_k.shape, lambda i: (0, 0)),
                  pl.BlockSpec(w_vt.shape, lambda i: (0, 0, 0)),
                  pl.BlockSpec((tm, LANE), lambda i: (i % nt, 0))],
        out_specs=[pl.BlockSpec((1, Hm, tm, QK_DIM), lambda i: (i // nt, 0, i % nt, 0)),
                   pl.BlockSpec((1, Hm, HEAD_DIM + V_PAD, tm), lambda i: (i // nt, 0, 0, i % nt))],
        out_shape=[jax.ShapeDtypeStruct((B, Hm, T, QK_DIM), BF16),
                   jax.ShapeDtypeStruct((B, Hm, HEAD_DIM + V_PAD, T), BF16)],
        compiler_params=_params("parallel"),
        name="mla_keys_values",
    )(u, u, norm_g.reshape(1, R), w_k, w_vt, tab)


def _attn_kernel(*refs, nseg, nsub):
    q_ref = refs[0]
    k_refs = refs[1:1 + nseg]
    vt_refs = refs[1 + nseg:1 + 2 * nseg]
    o_ref = refs[1 + 2 * nseg]
    rs = q_ref.shape[2] // nsub

    def scores(i):
        q = q_ref[0, 0, i * rs:(i + 1) * rs, :]
        return [_dot_nt(k[0, 0], q) for k in k_refs]

    def finish(i, s):
        m = functools.reduce(jnp.maximum, [jnp.max(x, axis=0, keepdims=True) for x in s])
        acc = functools.reduce(jnp.add, [_dot(vt[0, 0], jnp.exp2(x - m).astype(BF16)) for x, vt in zip(s, vt_refs)])
        o = acc[0:HEAD_DIM] / acc[HEAD_DIM:HEAD_DIM + 1]
        o_ref[i * rs:(i + 1) * rs, :] = o.T.astype(o_ref.dtype)

    s_next = scores(0)
    for i in range(nsub):
        s_cur = s_next
        if i + 1 < nsub:
            s_next = scores(i + 1)
        finish(i, s_cur)


def attention(q, ks, vts):
    B, Hm, Tq, _ = q.shape
    tq = _pick(Tq, (1024, 512, 256, 128))
    nq = Tq // tq
    nseg = len(ks)
    kspec = lambda a: pl.BlockSpec((1, 1) + a.shape[2:], lambda b, h, i: (b, h, 0, 0))
    return pl.pallas_call(
        functools.partial(_attn_kernel, nseg=nseg, nsub=max(1, tq // 256)),
        grid=(B, Hm, nq),
        in_specs=[pl.BlockSpec((1, 1, tq, QK_DIM), lambda b, h, i: (b, h, i, 0))]
                 + [kspec(a) for a in ks] + [kspec(a) for a in vts],
        out_specs=pl.BlockSpec((tq, HEAD_DIM), lambda b, h, i: (b * nq + i, h)),
        out_shape=jax.ShapeDtypeStruct((B * Tq, Hm * HEAD_DIM), BF16),
        compiler_params=_params("parallel", "parallel", "parallel"),
        name=f"attention_{nseg}seg",
    )(q, *ks, *vts)


def _merge_kernel(ac_ref, ag_ref, am_ref, wc_ref, wg_ref, wm_ref, gc_ref, gg_ref, gm_ref, o_ref):
    sig = lambda r: jax.nn.sigmoid(r[...].astype(F32))
    m = (sig(gc_ref) * _dot(ac_ref[...], wc_ref[0]) + sig(gg_ref) * _dot(ag_ref[...], wg_ref[0])
         + sig(gm_ref) * _dot(am_ref[...], wm_ref[0]))
    o_ref[...] = m.astype(o_ref.dtype)


def merge_branches(ac, ag, am, wc, wg, wm, l, u, off):
    N = ac.shape[0]
    D = wc.shape[2]
    tm = _pick(N, (512, 256, 128))
    tn = _pick((D, off["gate_conv"], off["gate_gdn"], off["gate_mla"]), (2048, 1024, 512, 256, 128))
    act = lambda a: pl.BlockSpec((tm, a.shape[1]), lambda i, j: (i, 0))
    resident = pl.Buffered(1) if tn == D else None
    wsp = lambda w: pl.BlockSpec((1, w.shape[1], tn), lambda i, j: (l, 0, j), pipeline_mode=resident)
    gate = lambda name: pl.BlockSpec((tm, tn), lambda i, j: (i, off[name] // tn + j))
    return pl.pallas_call(
        _merge_kernel,
        grid=(N // tm, D // tn),
        in_specs=[act(ac), act(ag), act(am), wsp(wc), wsp(wg), wsp(wm),
                  gate("gate_conv"), gate("gate_gdn"), gate("gate_mla")],
        out_specs=pl.BlockSpec((tm, tn), lambda i, j: (i, j)),
        out_shape=jax.ShapeDtypeStruct((N, D), BF16),
        compiler_params=_params("parallel", "parallel"),
        name="merge_branches",
    )(ac, ag, am, wc, wg, wm, u, u, u)


def _out_router_kernel(m_ref, w_ref, h_ref, g1_ref, n2_ref, sh_ref, sc_ref, r_ref, o_ref, hn_ref, lg_ref):
    h_new = h_ref[...] + g1_ref[0] * _dot(m_ref[...], w_ref[0])
    o_ref[...] = h_new
    a = _norm_mod(h_new, n2_ref[...], sh_ref[0], sc_ref[0]).astype(BF16)
    hn_ref[...] = a
    lg_ref[0] = _dot_nt(r_ref[0], a)


def out_projection_router(m, w_o, l, h, gate, norm_g, sh, sc, router_t, B, T, rows_per_group):
    N, D = h.shape
    E = router_t.shape[1]
    tm = _pick(T, (512, 256, 128))
    nt = T // tm
    gdiv = rows_per_group // tm
    grp = pl.BlockSpec((1, 1, D), lambda i: (i // gdiv, 0, 0))
    return pl.pallas_call(
        _out_router_kernel,
        grid=(N // tm,),
        in_specs=[pl.BlockSpec((tm, m.shape[1]), lambda i: (i, 0)),
                  pl.BlockSpec((1, m.shape[1], D), lambda i: (l, 0, 0), pipeline_mode=pl.Buffered(1)),
                  pl.BlockSpec((tm, D), lambda i: (i, 0)),
                  grp, pl.BlockSpec((1, D), lambda i: (0, 0)), grp, grp,
                  pl.BlockSpec((1, E, D), lambda i: (l, 0, 0))],
        out_specs=[pl.BlockSpec((tm, D), lambda i: (i, 0)),
                   pl.BlockSpec((tm, D), lambda i: (i, 0)),
                   pl.BlockSpec((1, E, tm), lambda i: (i // nt, 0, i % nt))],
        out_shape=[jax.ShapeDtypeStruct((N, D), F32), jax.ShapeDtypeStruct((N, D), BF16),
                   jax.ShapeDtypeStruct((B, E, T), F32)],
        input_output_aliases={2: 0} if l > 0 else {},
        compiler_params=_params("parallel"),
        name="out_projection_router",
    )(m, w_o, h, gate, norm_g.reshape(1, D), sh, sc, router_t)


def _route_kernel(lg_ref, rm_ref, aff_ref, rmt_ref, *, T, E, cap, W):
    lg = lg_ref[0]
    mx = jnp.max(lg, axis=0, keepdims=True)
    ex = jnp.exp(lg - mx)
    aff = ex / jnp.sum(ex, axis=0, keepdims=True)
    aff_ref[0] = aff
    bits = lax.bitcast_convert_type(aff, jnp.int32)

    def count(mask):
        return jnp.sum(jnp.where(mask, 1.0, 0.0), axis=1, keepdims=True)

    thr = jnp.zeros((E, 1), jnp.int32)
    for bit in range(30, -1, -1):
        cand = thr | (1 << bit)
        thr = jnp.where(count(bits >= cand) >= cap, cand, thr)
    gt = bits > thr
    eq = bits == thr
    need = cap - count(gt)

    ri = lax.broadcasted_iota(jnp.int32, (W, W), 0)
    ci = lax.broadcasted_iota(jnp.int32, (W, W), 1)
    upper = jnp.where(ri < ci, 1.0, 0.0).astype(BF16)
    ident = jnp.where(ri == ci, 1.0, 0.0).astype(BF16)

    def excl_cumsum(mask):
        m = jnp.where(mask, 1.0, 0.0)
        outs, carry = [], jnp.zeros((E, 1), F32)
        for j in range(T // W):
            blk = m[:, j * W:(j + 1) * W]
            outs.append(_dot(blk.astype(BF16), upper) + carry)
            carry = carry + jnp.sum(blk, axis=1, keepdims=True)
        return jnp.concatenate(outs, axis=1) if len(outs) > 1 else outs[0]

    sel = gt | (eq & (excl_cumsum(eq) < need))
    rm = jnp.where(sel, excl_cumsum(sel), -1.0)
    rm_ref[0] = rm
    rmb = rm.astype(BF16)
    for j in range(T // W):
        rmt_ref[0, j * W:(j + 1) * W, :] = _dot_nt(ident, rmb[:, j * W:(j + 1) * W])


def route(logits_t, cap):
    B, E, T = logits_t.shape
    W = 256 if T % 256 == 0 else 128
    blk = pl.BlockSpec((1, E, T), lambda b: (b, 0, 0))
    return pl.pallas_call(
        functools.partial(_route_kernel, T=T, E=E, cap=cap, W=W),
        grid=(B,),
        in_specs=[blk],
        out_specs=[blk, blk, pl.BlockSpec((1, T, E), lambda b: (b, 0, 0))],
        out_shape=[jax.ShapeDtypeStruct((B, E, T), F32), jax.ShapeDtypeStruct((B, E, T), F32),
                   jax.ShapeDtypeStruct((B, T, E), F32)],
        compiler_params=_params("parallel"),
        name="route",
    )(logits_t)


def _gather_kernel(rm_ref, aff_ref, hn_ref, xs_ref, gate_ref, *, cap):
    T = rm_ref.shape[3]
    slot = lax.broadcasted_iota(jnp.int32, (cap, T), 0).astype(F32)
    for e in range(rm_ref.shape[1]):
        hit = slot == rm_ref[0, e]
        xs_ref[0, e] = _dot(jnp.where(hit, 1.0, 0.0).astype(BF16), hn_ref[...]).astype(xs_ref.dtype)
        gate_ref[0, e] = jnp.sum(jnp.where(hit, aff_ref[0, e], 0.0), axis=1, keepdims=True)


def moe_gather(rm, aff, hn, cap):
    B, E, T = rm.shape
    D = hn.shape[1]
    eg = 2 if E % 2 == 0 else 1
    rowspec = pl.BlockSpec((1, eg, 1, T), lambda b, e: (b, e, 0, 0))
    return pl.pallas_call(
        functools.partial(_gather_kernel, cap=cap),
        grid=(B, E // eg),
        in_specs=[rowspec, rowspec, pl.BlockSpec((T, D), lambda b, e: (b, 0))],
        out_specs=[pl.BlockSpec((1, eg, cap, D), lambda b, e: (b, e, 0, 0)),
                   pl.BlockSpec((1, eg, cap, 1), lambda b, e: (b, e, 0, 0))],
        out_shape=[jax.ShapeDtypeStruct((B, E, cap, D), BF16), jax.ShapeDtypeStruct((B, E, cap, 1), F32)],
        compiler_params=_params("parallel", "parallel"),
        name="moe_gather",
    )(rm.reshape(B, E, 1, T), aff.reshape(B, E, 1, T), hn)


def _ffn_kernel(xs_ref, gate_ref, wg_ref, wu_ref, wd_ref, ye_ref):
    bb, _, cap, D = xs_ref.shape
    xs = xs_ref[...].reshape(bb * cap, D)
    hid = (_silu(_dot(xs, wg_ref[0, 0])) * _dot(xs, wu_ref[0, 0])).astype(BF16)
    ye = _dot(hid, wd_ref[0, 0]) * gate_ref[...].reshape(bb * cap, 1)
    ye_ref[...] = ye.reshape(bb, 1, cap, D).astype(ye_ref.dtype)


def moe_ffn(xs, gate, w_gate, w_up, w_down, l):
    B, E, cap, D = xs.shape
    FF = w_gate.shape[3]
    bb = max(1, min(B, 512 // cap))
    while B % bb:
        bb -= 1
    tok = lambda last: pl.BlockSpec((bb, 1, cap, last), lambda e, b: (b, e, 0, 0))
    return pl.pallas_call(
        _ffn_kernel,
        grid=(E, B // bb),
        in_specs=[tok(D), tok(1),
                  pl.BlockSpec((1, 1, D, FF), lambda e, b: (l, e, 0, 0)),
                  pl.BlockSpec((1, 1, D, FF), lambda e, b: (l, e, 0, 0)),
                  pl.BlockSpec((1, 1, FF, D), lambda e, b: (l, e, 0, 0))],
        out_specs=tok(D),
        out_shape=jax.ShapeDtypeStruct((B, E, cap, D), BF16),
        compiler_params=_params("parallel", "parallel"),
        name="moe_ffn",
    )(xs, gate, w_gate, w_up, w_down)


def _scatter_kernel(rmt_ref, ye_ref, h_ref, g_ref, o_ref, *, E, cap):
    tt = rmt_ref.shape[1]
    rmt = rmt_ref[0]
    slot = lax.broadcasted_iota(jnp.int32, (tt, cap), 1).astype(F32)
    acc = jnp.zeros(o_ref.shape, F32)
    for e in range(E):
        onehot = jnp.where(rmt[:, e:e + 1] == slot, 1.0, 0.0).astype(BF16)
        acc = acc + _dot(onehot, ye_ref[0, e * cap:(e + 1) * cap, :])
    o_ref[...] = h_ref[...] + g_ref[0] * acc


def moe_scatter(rmt, ye, h, gate, T):
    B, _, E = rmt.shape
    _, EC, D = ye.shape
    cap = EC // E
    tt = _pick(T, (512, 256, 128))
    tn = _pick(D, (1024, 512, 256, 128))
    nt = T // tt
    return pl.pallas_call(
        functools.partial(_scatter_kernel, E=E, cap=cap),
        grid=(B, D // tn, nt),
        in_specs=[pl.BlockSpec((1, tt, E), lambda b, j, i: (b, i, 0)),
                  pl.BlockSpec((1, EC, tn), lambda b, j, i: (b, 0, j)),
                  pl.BlockSpec((tt, tn), lambda b, j, i: (b * nt + i, j)),
                  pl.BlockSpec((1, 1, tn), lambda b, j, i: (b, 0, j))],
        out_specs=pl.BlockSpec((tt, tn), lambda b, j, i: (b * nt + i, j)),
        out_shape=jax.ShapeDtypeStruct(h.shape, F32),
        input_output_aliases={2: 0},
        compiler_params=_params("parallel", "parallel", "parallel"),
        name="moe_scatter",
    )(rmt, ye, h, gate)


def expert_choice_moe(h, hn, logits_t, gate, w_gate, w_up, w_down, l, B, T):
    E = logits_t.shape[1]
    cap = EC_CAPACITY * T // E
    assert cap <= 256 and cap % 16 == 0, "slot indices must stay exact in bf16 and fill bf16 sublane tiles"
    rm, aff, rmt = route(logits_t, cap)
    xs, gsl = moe_gather(rm, aff, hn, cap)
    ye = moe_ffn(xs, gsl, w_gate, w_up, w_down, l)
    return moe_scatter(rmt, ye.reshape(B, E * cap, ye.shape[-1]), h, gate, T)


def _final_norm_kernel(x_ref, g_ref, o_ref):
    o_ref[...] = _rms(x_ref[...], g_ref[...])


def final_norm(x, g):
    N, D = x.shape
    tm = _pick(N, (512, 256, 128))
    return pl.pallas_call(
        _final_norm_kernel,
        grid=(N // tm,),
        in_specs=[pl.BlockSpec((tm, D), lambda i: (i, 0)), pl.BlockSpec((1, D), lambda i: (0, 0))],
        out_specs=pl.BlockSpec((tm, D), lambda i: (i, 0)),
        out_shape=jax.ShapeDtypeStruct((N, D), F32),
        compiler_params=_params("parallel"),
        name="final_norm",
    )(x, g.reshape(1, D))


def _swap_rope(w):
    half = ROPE_DIM // 2
    return jnp.concatenate([-w[..., half:], w[..., :half]], axis=-1)


_PACKED_ORDER = ("gate_conv", "gate_gdn", "gate_mla", "mla_cq", "gdn_small", "mla_kpe", "mla_ckv", "conv_x", "conv_b",
                 "conv_c", "gdn_q", "gdn_k", "gdn_v", "gdn_z")


def _in_layout(C, Hg, RQ, RKV, D):
    width = {"mla_cq": RQ, "gdn_small": LANE, "mla_kpe": LANE, "mla_ckv": RKV, "conv_x": C, "conv_b": C, "conv_c": C,
             "gdn_q": Hg * HEAD_DIM, "gdn_k": Hg * HEAD_DIM, "gdn_v": Hg * HEAD_DIM, "gdn_z": Hg * HEAD_DIM,
             "gate_conv": D, "gate_gdn": D, "gate_mla": D}
    off, o = {}, 0
    for name in _PACKED_ORDER:
        off[name] = o
        o += width[name]
    assert off["mla_cq"] % RQ == 0 and off["mla_ckv"] % RKV == 0
    return off, o


def _pack_w_in(w, C, Hg, RQ, RKV, D):
    names = ("conv_x", "conv_b", "conv_c", "gdn_q", "gdn_k", "gdn_v", "gdn_z", "gdn_small", "mla_cq", "mla_ckv",
             "mla_kpe", "gate_conv", "gate_gdn", "gate_mla")
    sizes = (C, C, C, Hg * HEAD_DIM, Hg * HEAD_DIM, Hg * HEAD_DIM, Hg * HEAD_DIM, 4 * Hg, RQ, RKV, ROPE_DIM, D, D, D)
    w = w.astype(BF16)
    seg, o = {}, 0
    for name, size in zip(names, sizes):
        seg[name] = w[..., o:o + size]
        o += size
    seg["gdn_small"] = jnp.pad(seg["gdn_small"], ((0, 0), (0, 0), (0, LANE - 4 * Hg)))
    seg["mla_kpe"] = jnp.concatenate([seg["mla_kpe"], _swap_rope(seg["mla_kpe"])], axis=-1)
    return jnp.concatenate([seg[n] for n in _PACKED_ORDER], axis=-1)


def _rope_tables(T):
    rows = T // GRID_W
    row = jnp.repeat(jnp.arange(rows), GRID_W)
    col = jnp.tile(jnp.arange(GRID_W), rows)
    n_freq = ROPE_DIM // 4
    inv_freq = 1.0 / (ROPE_THETA ** (jnp.arange(n_freq, dtype=F32) / n_freq))
    ang = jnp.concatenate([row[:, None] * inv_freq, col[:, None] * inv_freq], axis=-1)
    cos, sin = jnp.cos(ang), jnp.sin(ang)
    return jnp.concatenate([cos, cos], axis=1), jnp.concatenate([sin, sin], axis=1)


def kernel(x, c, ctx, c_ctx, w_ada, b_ada, norm1_g, w_in, conv_w, conv_out, gdn_conv_w, gdn_a_log, gdn_dt_bias,
           gdn_norm_g, gdn_out, mla_q_norm_g, mla_w_uq, mla_kv_norm_g, mla_w_ukv, mla_out, w_o, norm2_g, router_w,
           w_gate, w_up, w_down, final_g):
    B, T, D = x.shape
    TC = ctx.shape[1]
    L = w_ada.shape[0]
    C = conv_w.shape[-1]
    Hg = gdn_a_log.shape[-1]
    RQ = mla_q_norm_g.shape[-1]
    RKV = mla_kv_norm_g.shape[-1]
    Hm = mla_w_uq.shape[-1] // QK_DIM
    assert 4 * Hg <= LANE and T % GDN_CHUNK == 0 and TC % GDN_CHUNK == 0

    off, _ = _in_layout(C, Hg, RQ, RKV, D)

    R = -(-(B + 1) // 8) * 8
    cc = jnp.concatenate([c, c_ctx[None, :], jnp.zeros((R - B - 1, D), F32)], axis=0)
    mod = ada_modulation(cc, w_ada, b_ada)

    cos64, sin64 = _rope_tables(T)
    q_ct, q_st = jnp.tile(cos64, (1, Hm)), jnp.tile(sin64, (1, Hm))
    q_ct0, q_st0 = jnp.ones((TC, Hm * ROPE_DIM), F32), jnp.zeros((TC, Hm * ROPE_DIM), F32)
    k_tab = jnp.concatenate([cos64, sin64], 1)
    k_tab0 = jnp.concatenate([jnp.ones((TC, ROPE_DIM), F32), jnp.zeros((TC, ROPE_DIM), F32)], 1)

    h = x.reshape(B * T, D)
    hc = ctx.reshape(B * TC, D)
    zero_state = jnp.zeros((B, Hg, HEAD_DIM, HEAD_DIM), F32)

    w_in_p = _pack_w_in(w_in, C, Hg, RQ, RKV, D)
    wc_o, wg_o, wm_o, wo_b = (a.astype(BF16) for a in (conv_out, gdn_out, mla_out, w_o))
    router_t = jnp.swapaxes(router_w, 1, 2).astype(BF16)
    wgt, wup, wdn = w_gate.astype(BF16), w_up.astype(BF16), w_down.astype(BF16)

    for l in range(L):
        need_ctx = l < L - 1
        wq = mla_w_uq[l].reshape(RQ, Hm, QK_DIM)
        wq_pe = wq[..., HEAD_DIM:]
        wq_main = jnp.concatenate([wq[..., :HEAD_DIM].reshape(RQ, -1), wq_pe.reshape(RQ, -1)], 1).astype(BF16)
        wq_aux = _swap_rope(wq_pe).reshape(RQ, -1).astype(BF16)
        wkv = mla_w_ukv[l].reshape(RKV, Hm, 2 * HEAD_DIM)
        w_k = wkv[..., :HEAD_DIM].reshape(RKV, -1).astype(BF16)
        w_vt = wkv[..., HEAD_DIM:].transpose(1, 2, 0).astype(BF16)
        pad = jnp.zeros((LANE - 2 * Hg,), F32)
        alog_row = jnp.concatenate([gdn_a_log[l].reshape(-1), pad]).reshape(1, LANE)
        dtb_row = jnp.concatenate([gdn_dt_bias[l].reshape(-1), pad]).reshape(1, LANE)

        ml = mod[l, :B].reshape(B, 1, 6 * D)
        mc = mod[l, B:B + 1].reshape(1, 1, 6 * D)
        sh1, sc1, g1, sh2, sc2, g2 = (ml[..., i * D:(i + 1) * D] for i in range(6))
        csh1, csc1, cg1, csh2, csc2, cg2 = (mc[..., i * D:(i + 1) * D] for i in range(6))

        u_lat = in_projection(norm_modulate(h, norm1_g[l], sh1, sc1, T), w_in_p, l)
        u_ctx = in_projection(norm_modulate(hc, norm1_g[l], csh1, csc1, B * TC), w_in_p, l)

        ag_ctx, s_f, s_b = gdn_mixer(u_ctx, gdn_conv_w[l], alog_row, dtb_row, gdn_norm_g[l], zero_state, zero_state,
                                     B, TC, Hg, off)
        ag_lat, _, _ = gdn_mixer(u_lat, gdn_conv_w[l], alog_row, dtb_row, gdn_norm_g[l], s_f, s_b, B, T, Hg, off)

        k_ctx, v_ctx = mla_keys_values(u_ctx, mla_kv_norm_g[l], w_k, w_vt, k_tab0, B, TC, Hm, off)
        k_lat, v_lat = mla_keys_values(u_lat, mla_kv_norm_g[l], w_k, w_vt, k_tab, B, T, Hm, off)
        q_lat = mla_queries(u_lat, mla_q_norm_g[l], wq_main, wq_aux, q_ct, q_st, B, T, Hm, off)
        am_lat = attention(q_lat, [k_ctx, k_lat], [v_ctx, v_lat])

        ac_lat = conv_mixer_front(u_lat, conv_w[l], B, T, off)
        m_lat = merge_branches(ac_lat, ag_lat, am_lat, wc_o, wg_o, wm_o, l, u_lat, off)
        h, hn, lg = out_projection_router(m_lat, wo_b, l, h, g1, norm2_g[l], sh2, sc2, router_t, B, T, T)
        h = expert_choice_moe(h, hn, lg, g2, wgt, wup, wdn, l, B, T)

        if need_ctx:
            q_ctx = mla_queries(u_ctx, mla_q_norm_g[l], wq_main, wq_aux, q_ct0, q_st0, B, TC, Hm, off)
            am_ctx = attention(q_ctx, [k_ctx], [v_ctx])
            ac_ctx = conv_mixer_front(u_ctx, conv_w[l], B, TC, off)
            m_ctx = merge_branches(ac_ctx, ag_ctx, am_ctx, wc_o, wg_o, wm_o, l, u_ctx, off)
            hc, hn, lg = out_projection_router(m_ctx, wo_b, l, hc, cg1, norm2_g[l], csh2, csc2, router_t, B, TC,
                                               B * TC)
            hc = expert_choice_moe(hc, hn, lg, jnp.broadcast_to(cg2, (B, 1, D)), wgt, wup, wdn, l, B, TC)

    return final_norm(h, final_g).reshape(B, T, D)
```

```python
import functools
import math

import jax
import jax.numpy as jnp
from jax import lax
from jax.experimental import pallas as pl
from jax.experimental.pallas import tpu as pltpu

F32 = jnp.float32
BF16 = jnp.bfloat16

RMS_EPS = 1e-6
L2_EPS = 1e-6
ROPE_THETA = 10000.0
GRID_W = 64
HEAD_DIM = 128
ROPE_DIM = 64
QK_DIM = HEAD_DIM + ROPE_DIM
GDN_CHUNK = 128
GDN_SUB = 16
EC_CAPACITY = 2
LANE = 128
V_PAD = 16
ATTN_LOGIT_SCALE =QK_DIM ** -0.5 * math.log2(math.e)
VMEM_LIMIT = 56 * 1024 * 1024


def _pick(n, cands):
    ns = n if isinstance(n, tuple) else (n,)
    for c in cands:
        if all(v % c == 0 for v in ns):
            return c
    raise ValueError(f"no tile in {cands} divides {ns}")


def _params(*sem):
    return pltpu.CompilerParams(dimension_semantics=sem, vmem_limit_bytes=VMEM_LIMIT)


def _dot(a, b):
    return jnp.dot(a, b, preferred_element_type=F32)


def _dot_nt(a, b):
    return lax.dot_general(a, b, (((1,), (1,)), ((), ())), preferred_element_type=F32)


def _dot_tn(a, b):
    return lax.dot_general(a, b, (((0,), (0,)), ((), ())), preferred_element_type=F32)


def _split2(x):
    hi = x.astype(BF16)
    lo = (x - hi.astype(F32)).astype(BF16)
    return hi, lo


def _split3(x):
    hi = x.astype(BF16)
    r = x - hi.astype(F32)
    mid = r.astype(BF16)
    lo = (r - mid.astype(F32)).astype(BF16)
    return hi, mid, lo


def _silu(x):
    return x * jax.nn.sigmoid(x)


def _softplus(x):
    return jnp.maximum(x, 0.0) + jnp.log(1.0 + jnp.exp(-jnp.abs(x)))


def _ada_kernel(c_ref, w_ref, b_ref, o_ref):
    a = _silu(c_ref[...])
    ah, al = _split2(a)
    wh, wl = _split2(w_ref[0])
    o_ref[0] = _dot(ah, wh) + (_dot(ah, wl) + _dot(al, wh)) + b_ref[0]


def ada_modulation(cc, w_ada, b_ada):
    L, D, D6 = w_ada.shape
    R = cc.shape[0]
    tn = _pick(D6, (1024, 512, 256, 128))
    return pl.pallas_call(
        _ada_kernel,
        grid=(L, D6 // tn),
        in_specs=[pl.BlockSpec((R, D), lambda l, j: (0, 0)),
                  pl.BlockSpec((1, D, tn), lambda l, j: (l, 0, j)),
                  pl.BlockSpec((1, 1, tn), lambda l, j: (l, 0, j))],
        out_specs=pl.BlockSpec((1, R, tn), lambda l, j: (l, 0, j)),
        out_shape=jax.ShapeDtypeStruct((L, R, D6), F32),
        compiler_params=_params("parallel", "parallel"),
        name="ada_modulation",
    )(cc, w_ada, b_ada.reshape(L, 1, D6))


def _norm_mod(x, g, sh, sc):
    ms = jnp.mean(x * x, axis=-1, keepdims=True)
    y = x * lax.rsqrt(ms + RMS_EPS) * g
    return y * (1.0 + sc) + sh


def _norm_mod_kernel(x_ref, g_ref, sh_ref, sc_ref, o_ref):
    o_ref[...] = _norm_mod(x_ref[...], g_ref[...], sh_ref[0], sc_ref[0]).astype(o_ref.dtype)


def norm_modulate(x, g, sh, sc, rows_per_group):
    N, D = x.shape
    tm = _pick(rows_per_group, (512, 256, 128))
    gdiv = rows_per_group // tm
    return pl.pallas_call(
        _norm_mod_kernel,
        grid=(N // tm,),
        in_specs=[pl.BlockSpec((tm, D), lambda i: (i, 0)),
                  pl.BlockSpec((1, D), lambda i: (0, 0)),
                  pl.BlockSpec((1, 1, D), lambda i: (i // gdiv, 0, 0)),
                  pl.BlockSpec((1, 1, D), lambda i: (i // gdiv, 0, 0))],
        out_specs=pl.BlockSpec((tm, D), lambda i: (i, 0)),
        out_shape=jax.ShapeDtypeStruct((N, D), BF16),
        compiler_params=_params("parallel"),
        name="norm_modulate",
    )(x, g.reshape(1, D), sh, sc)


def _in_proj_kernel(a_ref, w_ref, o_ref):
    o_ref[...] = _dot(a_ref[...], w_ref[0]).astype(o_ref.dtype)


def in_projection(a, w, l):
    N, D = a.shape
    NC = w.shape[2]
    tm = _pick(N, (2048, 1024, 512, 256, 128))
    tn = _pick(NC, (512, 256, 128))
    return pl.pallas_call(
        _in_proj_kernel,
        grid=(N // tm, NC // tn),
        in_specs=[pl.BlockSpec((tm, D), lambda i, j: (i, 0)),
                  pl.BlockSpec((1, D, tn), lambda i, j: (l, 0, j))],
        out_specs=pl.BlockSpec((tm, tn), lambda i, j: (i, j)),
        out_shape=jax.ShapeDtypeStruct((N, NC), BF16),
        compiler_params=_params("parallel", "parallel"),
        name="in_projection",
    )(a, w)


def _conv3(x, w, T):
    row = lax.broadcasted_iota(jnp.int32, x.shape, 0)
    prev = jnp.where(row == 0, 0.0, pltpu.roll(x, 1, 0))
    nxt = jnp.where(row == T - 1, 0.0, pltpu.roll(x, T - 1, 0))
    return prev * w[0:1] + x * w[1:2] + nxt * w[2:3]


def _conv_mixer_kernel(x_ref, b_ref, c_ref, w_ref, o_ref, *, T):
    v = c_ref[...].astype(F32) * x_ref[...].astype(F32)
    o_ref[...] = (b_ref[...].astype(F32) * _conv3(v, w_ref[...], T)).astype(o_ref.dtype)


def conv_mixer_front(u, conv_w, B, T, off):
    C = conv_w.shape[1]
    tc = _pick(C, (256, 128))
    ox, ob, oc = (off[k] // tc for k in ("conv_x", "conv_b", "conv_c"))
    return pl.pallas_call(
        functools.partial(_conv_mixer_kernel, T=T),
        grid=(B, C // tc),
        in_specs=[pl.BlockSpec((T, tc), lambda b, j: (b, ox + j)),
                  pl.BlockSpec((T, tc), lambda b, j: (b, ob + j)),
                  pl.BlockSpec((T, tc), lambda b, j: (b, oc + j)),
                  pl.BlockSpec((3, tc), lambda b, j: (0, j))],
        out_specs=pl.BlockSpec((T, tc), lambda b, j: (b, j)),
        out_shape=jax.ShapeDtypeStruct((B * T, C), BF16),
        compiler_params=_params("parallel", "parallel"),
        name="conv_mixer_front",
    )(u, u, u, conv_w)


def _gdn_solve(a_list, rhs_list, sub_mask, eye):
    g = a_list[0].shape[0]
    b16 = lambda ts: [t.astype(BF16) for t in ts]
    dots = lambda xs, ys: [_dot(x, y) for x, y in zip(xs, ys)]
    d = [jnp.where(sub_mask, a, 0.0) for a in a_list]
    n = [a - dd for a, dd in zip(a_list, d)]
    db = b16(d)
    d2b = b16(dots(db, db))
    d4b = b16(dots(d2b, d2b))
    d8b = b16(dots(d4b, d4b))
    x = [eye - dd for dd in d]
    for p in (d2b, d4b, d8b):
        x = [xx + t for xx, t in zip(x, dots(b16(x), p))]
    z = dots(b16(x), b16([jnp.concatenate([nn, r], axis=1) for nn, r in zip(n, rhs_list)]))
    mb = b16([zz[:, :g] for zz in z])
    zr = [zz[:, g:] for zz in z]
    powers = [mb]
    for _ in range(int(math.log2(GDN_CHUNK // GDN_SUB)) - 1):
        powers.append(b16(dots(powers[-1], powers[-1])))
    for p in reversed(powers[1:]):
        zr = [r + t for r, t in zip(zr, dots(p, b16(zr)))]
    return [r - t for r, t in zip(zr, dots(mb, b16(zr)))]


def _gdn_kernel(q_ref, k_ref, v_ref, z_ref, ab_ref, wq_ref, wk_ref, wv_ref, alog_ref, dtb_ref, ng_ref,
                s0f_ref, s0b_ref, o_ref, sf_ref, sb_ref,
                q_s, k_s, v_s, col_s, u_s, w_s, qg_s, kd_s, in_s, eg_s, o_s, st_s, *, T, H, G, GI, HB):
    hb = pl.program_id(1)
    NC = T // GDN_CHUNK
    L = GDN_CHUNK

    def l2n(x):
        return x * lax.rsqrt(jnp.sum(x * x, axis=-1, keepdims=True) + L2_EPS)

    ri = lax.broadcasted_iota(jnp.int32, (G, G), 0)
    ci = lax.broadcasted_iota(jnp.int32, (G, G), 1)
    same_chunk = (ri >> int(math.log2(L))) == (ci >> int(math.log2(L)))
    same_sub = (ri >> int(math.log2(GDN_SUB))) == (ci >> int(math.log2(GDN_SUB)))
    eye = jnp.where(ri == ci, 1.0, 0.0).astype(F32)
    ones_chunk = jnp.where(same_chunk, 1.0, 0.0).astype(BF16)

    ab = ab_ref[...].astype(F32)
    lane = lax.broadcasted_iota(jnp.int32, ab.shape, 1)
    gfull = -jnp.exp(alog_ref[...]) * _softplus(ab + dtb_ref[...])
    bfull = jax.nn.sigmoid(ab)

    def col(x, idx):
        return jnp.sum(jnp.where(lane == idx, x, 0.0), axis=1, keepdims=True)

    for hh in range(HB):
        hsl = slice(hh * HEAD_DIM, (hh + 1) * HEAD_DIM)
        head = hb * HB + hh

        def prep(x_ref, w_ref):
            return _silu(_conv3(x_ref[:, hsl].astype(F32), w_ref[:, hsl], T))

        q_s[hh] = l2n(prep(q_ref, wq_ref)) * (HEAD_DIM ** -0.5)
        k_s[hh] = l2n(prep(k_ref, wk_ref))
        v_s[hh] = prep(v_ref, wv_ref)

        g_f, g_b = col(gfull, head), col(gfull, H + head)
        b_f, b_b = col(bfull, 2 * H + head), col(bfull, 3 * H + head)
        col_s[hh] = jnp.where(lane == 0, g_f, jnp.where(lane == 1, g_b, jnp.where(lane == 2, b_f, b_b)))
        st_s[hh, 0] = s0f_ref[0, hh]
        st_s[hh, 1] = s0b_ref[0, hh]

    chains = [(hh, d) for hh in range(HB) for d in range(2)]
    incl = [same_chunk & (ci <= ri), same_chunk & (ci >= ri)]
    strict = [same_chunk & (ci < ri), same_chunk & (ci > ri)]
    incl_b = [jnp.where(m, 1.0, 0.0).astype(BF16) for m in incl]
    ones_8g = jnp.ones((8, G), BF16)

    def exact3_list(ms, xs):
        parts = [_split3(x) for x in xs]
        return [_dot(m, p[0]) + _dot(m, p[1]) + _dot(m, p[2]) for m, p in zip(ms, parts)]

    def group_body(gi, carry):
        units = [(j, hh) for j in range(GI) for hh in range(HB)]
        gch = [(ui, d) for ui in range(len(units)) for d in range(2)]
        r0 = [pl.multiple_of((gi * GI + j) * G, G) for j in range(GI)]
        rows = [pl.ds(r0[j], G) for j, _ in units]
        qg = [q_s[hh, r, :] for (_, hh), r in zip(units, rows)]
        kg = [k_s[hh, r, :] for (_, hh), r in zip(units, rows)]
        vg = [v_s[hh, r, :] for (_, hh), r in zip(units, rows)]
        cols = [col_s[hh, r, :] for (_, hh), r in zip(units, rows)]
        kb16 = [k.astype(BF16) for k in kg]
        kk = [_dot_nt(kb, kb) for kb in kb16]
        qk = [_dot_nt(q.astype(BF16), kb) for q, kb in zip(qg, kb16)]
        bcol = [cols[ui][:, 2 + d:3 + d] for ui, d in gch]
        gb = [jnp.broadcast_to(cols[ui][:, d:d + 1], (G, LANE)) for ui, d in gch]
        gc = exact3_list([incl_b[d] for _, d in gch], gb)
        gl = exact3_list([ones_chunk] * len(gch), gb)
        cmat = [jnp.concatenate([g] * (G // LANE), axis=1) if G > LANE else g for g in gc]
        rrow = [r[0:1] for r in exact3_list([ones_8g] * len(gch),
                                            [jnp.where(ri == ci, cm, 0.0) for cm in cmat])]
        decay = [jnp.where(incl[d], jnp.exp(jnp.where(incl[d], cm - rr, 0.0)), 0.0)
                 for (_, d), cm, rr in zip(gch, cmat, rrow)]
        a = [jnp.where(strict[d], bc * kk[ui] * dc, 0.0) for (ui, d), bc, dc in zip(gch, bcol, decay)]
        egc = [jnp.exp(g) for g in gc]
        rhs = [jnp.concatenate([vg[ui] * bc, kg[ui] * bc * e], axis=1) for (ui, _), bc, e in zip(gch, bcol, egc)]
        uw = _gdn_solve(a, rhs, same_sub, eye)
        for i, (ui, d) in enumerate(gch):
            j, hh = units[ui]
            u_s[hh, d, rows[ui], :] = uw[i][:, :HEAD_DIM]
            w_s[hh, d, rows[ui], :] = uw[i][:, HEAD_DIM:].astype(BF16)
            qg_s[hh, d, rows[ui], :] = (qg[ui] * egc[i]).astype(BF16)
            kd_s[hh, d, rows[ui], :] = (kg[ui] * jnp.exp(gl[i] - gc[i])).astype(BF16)
            egl = jnp.exp(gl[i])
            intra = (qk[ui] * decay[i]).astype(BF16)
            for c in range(G // L):
                e0 = pl.multiple_of(((gi * GI + j) * (G // L) + c) * 8, 8)
                eg_s[hh, d, pl.ds(e0, 8), :] = egl[c * L:c * L + 8]
                in_s[hh, d, pl.ds(r0[j] + c * L, L), :] = intra[c * L:(c + 1) * L, c * L:(c + 1) * L]
        return carry

    lax.fori_loop(0, T // (G * GI), group_body, 0)

    def scan_body(c, carry):
        rws = [pl.ds(pl.multiple_of((c if d == 0 else NC - 1 - c) * L, L), L) for _, d in chains]
        egr = [pl.ds(pl.multiple_of((c if d == 0 else NC - 1 - c) * 8, 8), 8) for _, d in chains]
        s = [st_s[hh, d] for hh, d in chains]
        wq = [jnp.concatenate([w_s[hh, d, r, :], qg_s[hh, d, r, :]], axis=0) for (hh, d), r in zip(chains, rws)]
        ws = [_dot(x, y.astype(BF16)) for x, y in zip(wq, s)]
        vb = [(u_s[hh, d, r, :] - w[:L]).astype(BF16) for (hh, d), r, w in zip(chains, rws, ws)]
        oi = [_dot(in_s[hh, d, r, :], v) for (hh, d), r, v in zip(chains, rws, vb)]
        ds_ = [_dot_tn(kd_s[hh, d, r, :], v) for (hh, d), r, v in zip(chains, rws, vb)]
        for i, (hh, d) in enumerate(chains):
            o_s[hh, d, rws[i], :] = ws[i][L:] + oi[i]
            st_s[hh, d] = s[i] * eg_s[hh, d, egr[i], :][0:1] + ds_[i]
        return carry

    lax.fori_loop(0, NC, scan_body, 0)

    for hh in range(HB):
        hsl = slice(hh * HEAD_DIM, (hh + 1) * HEAD_DIM)
        sf_ref[0, hh] = st_s[hh, 0]
        sb_ref[0, hh] = st_s[hh, 1]
        o = o_s[hh, 0] + o_s[hh, 1]
        y = o * lax.rsqrt(jnp.mean(o * o, axis=-1, keepdims=True) + RMS_EPS) * ng_ref[...]
        o_ref[:, hsl] = (y * _silu(z_ref[:, hsl].astype(F32))).astype(o_ref.dtype)


def gdn_mixer(u, conv_w, alog_row, dtb_row, norm_g, s0f, s0b, B, T, H, off):
    G = GDN_CHUNK
    assert G == LANE
    GI = _pick(T // G, (4, 2, 1))
    HB = 2 if H % 2 == 0 else 1
    W = HB * HEAD_DIM
    assert all(off[k] % W == 0 for k in ("gdn_q", "gdn_k", "gdn_v", "gdn_z"))
    oq, ok, ov, oz = (off[k] // W for k in ("gdn_q", "gdn_k", "gdn_v", "gdn_z"))
    oab = off["gdn_small"] // LANE
    seq = lambda o: pl.BlockSpec((T, W), lambda b, h: (b, o + h))
    cw = lambda o: pl.BlockSpec((3, W), lambda b, h: (0, o * (H // HB) + h))
    row = pl.BlockSpec((1, LANE), lambda b, h: (0, 0))
    st = pl.BlockSpec((1, HB, HEAD_DIM, HEAD_DIM), lambda b, h: (b, h, 0, 0))
    tbuf = lambda: pltpu.VMEM((HB, T, HEAD_DIM), F32)
    hbuf = lambda dt: pltpu.VMEM((HB, 2, T, HEAD_DIM), dt)
    return pl.pallas_call(
        functools.partial(_gdn_kernel, T=T, H=H, G=G, GI=GI, HB=HB),
        grid=(B, H // HB),
        in_specs=[seq(oq), seq(ok), seq(ov), seq(oz),
                  pl.BlockSpec((T, LANE), lambda b, h: (b, oab)),
                  cw(0), cw(1), cw(2), row, row, row, st, st],
        out_specs=[pl.BlockSpec((T, W), lambda b, h: (b, h)), st, st],
        out_shape=[jax.ShapeDtypeStruct((B * T, H * HEAD_DIM), BF16),
                   jax.ShapeDtypeStruct((B, H, HEAD_DIM, HEAD_DIM), F32),
                   jax.ShapeDtypeStruct((B, H, HEAD_DIM, HEAD_DIM), F32)],
        scratch_shapes=[tbuf(), tbuf(), tbuf(), tbuf(), hbuf(F32), hbuf(BF16), hbuf(BF16), hbuf(BF16),
                        pltpu.VMEM((HB, 2, T, GDN_CHUNK), BF16), pltpu.VMEM((HB, 2, T // 8, HEAD_DIM), F32),
                        hbuf(F32), pltpu.VMEM((HB, 2, HEAD_DIM, HEAD_DIM), F32)],
        compiler_params=_params("parallel", "parallel"),
        name="gdn_mixer",
    )(u, u, u, u, u, conv_w, conv_w, conv_w, alog_row, dtb_row, norm_g.reshape(1, HEAD_DIM), s0f, s0b)


def _rms(x, g):
    return x * lax.rsqrt(jnp.mean(x * x, axis=-1, keepdims=True) + RMS_EPS) * g


def _mla_q_kernel(x_ref, g_ref, wm_ref, wa_ref, ct_ref, st_ref, o_ref, *, Hm):
    a = _rms(x_ref[...].astype(F32), g_ref[...]).astype(BF16)
    main = _dot(a, wm_ref[...]) * ATTN_LOGIT_SCALE
    nn = Hm * HEAD_DIM
    pe = main[:, nn:] * ct_ref[...] + (_dot(a, wa_ref[...]) * ATTN_LOGIT_SCALE) * st_ref[...]
    for h in range(Hm):
        o_ref[0, h, :, 0:HEAD_DIM] = main[:, h * HEAD_DIM:(h + 1) * HEAD_DIM].astype(o_ref.dtype)
        o_ref[0, h, :, HEAD_DIM:QK_DIM] = pe[:, h * ROPE_DIM:(h + 1) * ROPE_DIM].astype(o_ref.dtype)


def mla_queries(u, norm_g, w_main, w_aux, ctab, stab, B, T, Hm, off):
    R = w_main.shape[0]
    tm = _pick(T, (512, 256, 128))
    nt = T // tm
    oc = off["mla_cq"] // R
    full = lambda a: pl.BlockSpec(a.shape, lambda i: (0, 0))
    tab = pl.BlockSpec((tm, Hm * ROPE_DIM), lambda i: (i % nt, 0))
    return pl.pallas_call(
        functools.partial(_mla_q_kernel, Hm=Hm),
        grid=(B * nt,),
        in_specs=[pl.BlockSpec((tm, R), lambda i: (i, oc)), pl.BlockSpec((1, R), lambda i: (0, 0)),
                  full(w_main), full(w_aux), tab, tab],
        out_specs=pl.BlockSpec((1, Hm, tm, QK_DIM), lambda i: (i // nt, 0, i % nt, 0)),
        out_shape=jax.ShapeDtypeStruct((B, Hm, T, QK_DIM), BF16),
        compiler_params=_params("parallel"),
        name="mla_queries",
    )(u, norm_g.reshape(1, R), w_main, w_aux, ctab, stab)


def _mla_kv_kernel(x_ref, pe_ref, g_ref, wk_ref, wvt_ref, tab_ref, k_ref, vt_ref, *, Hm):
    a = _rms(x_ref[...].astype(F32), g_ref[...]).astype(BF16)
    kn = _dot(a, wk_ref[...])
    r = pe_ref[...].astype(F32) * tab_ref[...]
    pe = (r + pltpu.roll(r, ROPE_DIM, 1))[:, 0:ROPE_DIM].astype(k_ref.dtype)
    for h in range(Hm):
        k_ref[0, h, :, 0:HEAD_DIM] = kn[:, h * HEAD_DIM:(h + 1) * HEAD_DIM].astype(k_ref.dtype)
        k_ref[0, h, :, HEAD_DIM:QK_DIM] = pe
        vt_ref[0, h, 0:HEAD_DIM, :] = _dot_nt(wvt_ref[h], a).astype(vt_ref.dtype)
        vt_ref[0, h, HEAD_DIM:, :] = jnp.ones((V_PAD, a.shape[0]), vt_ref.dtype)


def mla_keys_values(u, norm_g, w_k, w_vt, tab, B, T, Hm, off):
    R = w_k.shape[0]
    tm = _pick(T, (512, 256, 128))
    nt = T // tm
    oc, op = off["mla_ckv"] // R, off["mla_kpe"] // LANE
    return pl.pallas_call(
        functools.partial(_mla_kv_kernel, Hm=Hm),
        grid=(B * nt,),
        in_specs=[pl.BlockSpec((tm, R), lambda i: (i, oc)),
                  pl.BlockSpec((tm, LANE), lambda i: (i, op)),
                  pl.BlockSpec((1, R), lambda i: (0, 0)),
                  pl.BlockSpec(w_k.shape, lambda i: (0, 0)),
                  pl.BlockSpec(w_vt.shape, lambda i: (0, 0, 0)),
                  pl.BlockSpec((tm, LANE), lambda i: (i % nt, 0))],
        out_specs=[pl.BlockSpec((1, Hm, tm, QK_DIM), lambda i: (i // nt, 0, i % nt, 0)),
                   pl.BlockSpec((1, Hm, HEAD_DIM + V_PAD, tm), lambda i: (i // nt, 0, 0, i % nt))],
        out_shape=[jax.ShapeDtypeStruct((B, Hm, T, QK_DIM), BF16),
                   jax.ShapeDtypeStruct((B, Hm, HEAD_DIM + V_PAD, T), BF16)],
        compiler_params=_params("parallel"),
        name="mla_keys_values",
    )(u, u, norm_g.reshape(1, R), w_k, w_vt, tab)


def _attn_kernel(*refs, nseg, nsub):
    q_ref = refs[0]
    k_refs = refs[1:1 + nseg]
    vt_refs = refs[1 + nseg:1 + 2 * nseg]
    o_ref = refs[1 + 2 * nseg]
    rs = q_ref.shape[2] // nsub

    def scores(i):
        q = q_ref[0, 0, i * rs:(i + 1) * rs, :]
        return [_dot_nt(k[0, 0], q) for k in k_refs]

    def finish(i, s):
        m = functools.reduce(jnp.maximum, [jnp.max(x, axis=0, keepdims=True) for x in s])
        acc = functools.reduce(jnp.add, [_dot(vt[0, 0], jnp.exp2(x - m).astype(BF16)) for x, vt in zip(s, vt_refs)])
        o = acc[0:HEAD_DIM] / acc[HEAD_DIM:HEAD_DIM + 1]
        o_ref[i * rs:(i + 1) * rs, :] = o.T.astype(o_ref.dtype)

    s_next = scores(0)
    for i in range(nsub):
        s_cur = s_next
        if i + 1 < nsub:
            s_next = scores(i + 1)
        finish(i, s_cur)


def attention(q, ks, vts):
    B, Hm, Tq, _ = q.shape
    tq = _pick(Tq, (2048, 1024, 512, 256, 128))
    nq = Tq // tq
    nseg = len(ks)
    kspec = lambda a: pl.BlockSpec((1, 1) + a.shape[2:], lambda b, h, i: (b, h, 0, 0))
    return pl.pallas_call(
        functools.partial(_attn_kernel, nseg=nseg, nsub=max(1, tq // 256)),
        grid=(B, Hm, nq),
        in_specs=[pl.BlockSpec((1, 1, tq, QK_DIM), lambda b, h, i: (b, h, i, 0))]
                 + [kspec(a) for a in ks] + [kspec(a) for a in vts],
        out_specs=pl.BlockSpec((tq, HEAD_DIM), lambda b, h, i: (b * nq + i, h)),
        out_shape=jax.ShapeDtypeStruct((B * Tq, Hm * HEAD_DIM), BF16),
        compiler_params=_params("parallel", "parallel", "parallel"),
        name=f"attention_{nseg}seg",
    )(q, *ks, *vts)


def _merge_kernel(ac_ref, ag_ref, am_ref, wc_ref, wg_ref, wm_ref, gc_ref, gg_ref, gm_ref, o_ref):
    sig = lambda r: jax.nn.sigmoid(r[...].astype(F32))
    m = (sig(gc_ref) * _dot(ac_ref[...], wc_ref[0]) + sig(gg_ref) * _dot(ag_ref[...], wg_ref[0])
         + sig(gm_ref) * _dot(am_ref[...], wm_ref[0]))
    o_ref[...] = m.astype(o_ref.dtype)


def merge_branches(ac, ag, am, wc, wg, wm, l, u, off):
    N = ac.shape[0]
    D = wc.shape[2]
    tm = _pick(N, (512, 256, 128))
    tn = _pick((D, off["gate_conv"], off["gate_gdn"], off["gate_mla"]), (2048, 1024, 512, 256, 128))
    act = lambda a: pl.BlockSpec((tm, a.shape[1]), lambda i, j: (i, 0))
    resident = pl.Buffered(1) if tn == D else None
    wsp = lambda w: pl.BlockSpec((1, w.shape[1], tn), lambda i, j: (l, 0, j), pipeline_mode=resident)
    gate = lambda name: pl.BlockSpec((tm, tn), lambda i, j: (i, off[name] // tn + j))
    return pl.pallas_call(
        _merge_kernel,
        grid=(N // tm, D // tn),
        in_specs=[act(ac), act(ag), act(am), wsp(wc), wsp(wg), wsp(wm),
                  gate("gate_conv"), gate("gate_gdn"), gate("gate_mla")],
        out_specs=pl.BlockSpec((tm, tn), lambda i, j: (i, j)),
        out_shape=jax.ShapeDtypeStruct((N, D), BF16),
        compiler_params=_params("parallel", "parallel"),
        name="merge_branches",
    )(ac, ag, am, wc, wg, wm, u, u, u)


def _out_router_kernel(m_ref, w_ref, h_ref, g1_ref, n2_ref, sh_ref, sc_ref, r_ref, o_ref, hn_ref, lg_ref):
    h_new = h_ref[...] + g1_ref[0] * _dot(m_ref[...], w_ref[0])
    o_ref[...] = h_new
    a = _norm_mod(h_new, n2_ref[...], sh_ref[0], sc_ref[0]).astype(BF16)
    hn_ref[...] = a
    lg_ref[0] = _dot_nt(r_ref[0], a)


def out_projection_router(m, w_o, l, h, gate, norm_g, sh, sc, router_t, B, T, rows_per_group):
    N, D = h.shape
    E = router_t.shape[1]
    tm = _pick(T, (512, 256, 128))
    nt = T // tm
    gdiv = rows_per_group // tm
    grp = pl.BlockSpec((1, 1, D), lambda i: (i // gdiv, 0, 0))
    return pl.pallas_call(
        _out_router_kernel,
        grid=(N // tm,),
        in_specs=[pl.BlockSpec((tm, m.shape[1]), lambda i: (i, 0)),
                  pl.BlockSpec((1, m.shape[1], D), lambda i: (l, 0, 0), pipeline_mode=pl.Buffered(1)),
                  pl.BlockSpec((tm, D), lambda i: (i, 0)),
                  grp, pl.BlockSpec((1, D), lambda i: (0, 0)), grp, grp,
                  pl.BlockSpec((1, E, D), lambda i: (l, 0, 0))],
        out_specs=[pl.BlockSpec((tm, D), lambda i: (i, 0)),
                   pl.BlockSpec((tm, D), lambda i: (i, 0)),
                   pl.BlockSpec((1, E, tm), lambda i: (i // nt, 0, i % nt))],
        out_shape=[jax.ShapeDtypeStruct((N, D), F32), jax.ShapeDtypeStruct((N, D), BF16),
                   jax.ShapeDtypeStruct((B, E, T), F32)],
        input_output_aliases={2: 0} if l > 0 else {},
        compiler_params=_params("parallel"),
        name="out_projection_router",
    )(m, w_o, h, gate, norm_g.reshape(1, D), sh, sc, router_t)


def _route_kernel(lg_ref, rm_ref, aff_ref, rmt_ref, *, T, E, cap, W):
    lg = lg_ref[0]
    mx = jnp.max(lg, axis=0, keepdims=True)
    ex = jnp.exp(lg - mx)
    aff = ex / jnp.sum(ex, axis=0, keepdims=True)
    aff_ref[0] = aff
    bits = lax.bitcast_convert_type(aff, jnp.int32)

    def count(mask):
        return jnp.sum(jnp.where(mask, 1.0, 0.0), axis=1, keepdims=True)

    thr = jnp.zeros((E, 1), jnp.int32)
    for bit in range(30, -1, -1):
        cand = thr | (1 << bit)
        thr = jnp.where(count(bits >= cand) >= cap, cand, thr)
    gt = bits > thr
    eq = bits == thr
    need = cap - count(gt)

    ri = lax.broadcasted_iota(jnp.int32, (W, W), 0)
    ci = lax.broadcasted_iota(jnp.int32, (W, W), 1)
    upper = jnp.where(ri < ci, 1.0, 0.0).astype(BF16)
    ident = jnp.where(ri == ci, 1.0, 0.0).astype(BF16)

    def excl_cumsum(mask):
        m = jnp.where(mask, 1.0, 0.0)
        outs, carry = [], jnp.zeros((E, 1), F32)
        for j in range(T // W):
            blk = m[:, j * W:(j + 1) * W]
            outs.append(_dot(blk.astype(BF16), upper) + carry)
            carry = carry + jnp.sum(blk, axis=1, keepdims=True)
        return jnp.concatenate(outs, axis=1) if len(outs) > 1 else outs[0]

    sel = gt | (eq & (excl_cumsum(eq) < need))
    rm = jnp.where(sel, excl_cumsum(sel), -1.0)
    rm_ref[0] = rm
    rmb = rm.astype(BF16)
    for j in range(T // W):
        rmt_ref[0, j * W:(j + 1) * W, :] = _dot_nt(ident, rmb[:, j * W:(j + 1) * W])


def route(logits_t, cap):
    B, E, T = logits_t.shape
    W = 256 if T % 256 == 0 else 128
    blk = pl.BlockSpec((1, E, T), lambda b: (b, 0, 0))
    return pl.pallas_call(
        functools.partial(_route_kernel, T=T, E=E, cap=cap, W=W),
        grid=(B,),
        in_specs=[blk],
        out_specs=[blk, blk, pl.BlockSpec((1, T, E), lambda b: (b, 0, 0))],
        out_shape=[jax.ShapeDtypeStruct((B, E, T), F32), jax.ShapeDtypeStruct((B, E, T), F32),
                   jax.ShapeDtypeStruct((B, T, E), F32)],
        compiler_params=_params("parallel"),
        name="route",
    )(logits_t)


def _gather_kernel(rm_ref, aff_ref, hn_ref, xs_ref, gate_ref, *, cap):
    T = rm_ref.shape[3]
    slot = lax.broadcasted_iota(jnp.int32, (cap, T), 0).astype(F32)
    for e in range(rm_ref.shape[1]):
        hit = slot == rm_ref[0, e]
        xs_ref[0, e] = _dot(jnp.where(hit, 1.0, 0.0).astype(BF16), hn_ref[...]).astype(xs_ref.dtype)
        gate_ref[0, e] = jnp.sum(jnp.where(hit, aff_ref[0, e], 0.0), axis=1, keepdims=True)


def moe_gather(rm, aff, hn, cap):
    B, E, T = rm.shape
    D = hn.shape[1]
    eg = _pick(E, (4, 2, 1))
    rowspec = pl.BlockSpec((1, eg, 1, T), lambda b, e: (b, e, 0, 0))
    return pl.pallas_call(
        functools.partial(_gather_kernel, cap=cap),
        grid=(B, E // eg),
        in_specs=[rowspec, rowspec, pl.BlockSpec((T, D), lambda b, e: (b, 0))],
        out_specs=[pl.BlockSpec((1, eg, cap, D), lambda b, e: (b, e, 0, 0)),
                   pl.BlockSpec((1, eg, cap, 1), lambda b, e: (b, e, 0, 0))],
        out_shape=[jax.ShapeDtypeStruct((B, E, cap, D), BF16), jax.ShapeDtypeStruct((B, E, cap, 1), F32)],
        compiler_params=_params("parallel", "parallel"),
        name="moe_gather",
    )(rm.reshape(B, E, 1, T), aff.reshape(B, E, 1, T), hn)


def _ffn_kernel(xs_ref, gate_ref, wg_ref, wu_ref, wd_ref, ye_ref):
    bb, _, cap, D = xs_ref.shape
    xs = xs_ref[...].reshape(bb * cap, D)
    hid = (_silu(_dot(xs, wg_ref[0, 0])) * _dot(xs, wu_ref[0, 0])).astype(BF16)
    ye = _dot(hid, wd_ref[0, 0]) * gate_ref[...].reshape(bb * cap, 1)
    ye_ref[...] = ye.reshape(bb, 1, cap, D).astype(ye_ref.dtype)


def moe_ffn(xs, gate, w_gate, w_up, w_down, l):
    B, E, cap, D = xs.shape
    FF = w_gate.shape[3]
    bb = max(1, min(B, 512 // cap))
    while B % bb:
        bb -= 1
    tok = lambda last: pl.BlockSpec((bb, 1, cap, last), lambda e, b: (b, e, 0, 0))
    return pl.pallas_call(
        _ffn_kernel,
        grid=(E, B // bb),
        in_specs=[tok(D), tok(1),
                  pl.BlockSpec((1, 1, D, FF), lambda e, b: (l, e, 0, 0)),
                  pl.BlockSpec((1, 1, D, FF), lambda e, b: (l, e, 0, 0)),
                  pl.BlockSpec((1, 1, FF, D), lambda e, b: (l, e, 0, 0))],
        out_specs=tok(D),
        out_shape=jax.ShapeDtypeStruct((B, E, cap, D), BF16),
        compiler_params=_params("parallel", "parallel"),
        name="moe_ffn",
    )(xs, gate, w_gate, w_up, w_down)


def _scatter_kernel(rmt_ref, ye_ref, h_ref, g_ref, o_ref, *, E, cap):
    tt = rmt_ref.shape[1]
    rmt = rmt_ref[0]
    slot = lax.broadcasted_iota(jnp.int32, (tt, cap), 1).astype(F32)
    acc = jnp.zeros(o_ref.shape, F32)
    for e in range(E):
        onehot = jnp.where(rmt[:, e:e + 1] == slot, 1.0, 0.0).astype(BF16)
        acc = acc + _dot(onehot, ye_ref[0, e * cap:(e + 1) * cap, :])
    o_ref[...] = h_ref[...] + g_ref[0] * acc


def moe_scatter(rmt, ye, h, gate, T):
    B, _, E = rmt.shape
    _, EC, D = ye.shape
    cap = EC // E
    tt = _pick(T, (512, 256, 128))
    tn = _pick(D, (1024, 512, 256, 128))
    nt = T // tt
    return pl.pallas_call(
        functools.partial(_scatter_kernel, E=E, cap=cap),
        grid=(B, D // tn, nt),
        in_specs=[pl.BlockSpec((1, tt, E), lambda b, j, i: (b, i, 0)),
                  pl.BlockSpec((1, EC, tn), lambda b, j, i: (b, 0, j)),
                  pl.BlockSpec((tt, tn), lambda b, j, i: (b * nt + i, j)),
                  pl.BlockSpec((1, 1, tn), lambda b, j, i: (b, 0, j))],
        out_specs=pl.BlockSpec((tt, tn), lambda b, j, i: (b * nt + i, j)),
        out_shape=jax.ShapeDtypeStruct(h.shape, F32),
        input_output_aliases={2: 0},
        compiler_params=_params("parallel", "parallel", "parallel"),
        name="moe_scatter",
    )(rmt, ye, h, gate)


def expert_choice_moe(h, hn, logits_t, gate, w_gate, w_up, w_down, l, B, T):
    E = logits_t.shape[1]
    cap = EC_CAPACITY * T // E
    assert cap <= 256 and cap % 16 == 0, "slot indices must stay exact in bf16 and fill bf16 sublane tiles"
    rm, aff, rmt = route(logits_t, cap)
    xs, gsl = moe_gather(rm, aff, hn, cap)
    ye = moe_ffn(xs, gsl, w_gate, w_up, w_down, l)
    return moe_scatter(rmt, ye.reshape(B, E * cap, ye.shape[-1]), h, gate, T)


def _final_norm_kernel(x_ref, g_ref, o_ref):
    o_ref[...] = _rms(x_ref[...], g_ref[...])


def final_norm(x, g):
    N, D = x.shape
    tm = _pick(N, (512, 256, 128))
    return pl.pallas_call(
        _final_norm_kernel,
        grid=(N // tm,),
        in_specs=[pl.BlockSpec((tm, D), lambda i: (i, 0)), pl.BlockSpec((1, D), lambda i: (0, 0))],
        out_specs=pl.BlockSpec((tm, D), lambda i: (i, 0)),
        out_shape=jax.ShapeDtypeStruct((N, D), F32),
        compiler_params=_params("parallel"),
        name="final_norm",
    )(x, g.reshape(1, D))


def _swap_rope(w):
    half = ROPE_DIM // 2
    return jnp.concatenate([-w[..., half:], w[..., :half]], axis=-1)


_PACKED_ORDER = ("gate_conv", "gate_gdn", "gate_mla", "mla_cq", "gdn_small", "mla_kpe", "mla_ckv", "conv_x", "conv_b",
                 "conv_c", "gdn_q", "gdn_k", "gdn_v", "gdn_z")


def _in_layout(C, Hg, RQ, RKV, D):
    width = {"mla_cq": RQ, "gdn_small": LANE, "mla_kpe": LANE, "mla_ckv": RKV, "conv_x": C, "conv_b": C, "conv_c": C,
             "gdn_q": Hg * HEAD_DIM, "gdn_k": Hg * HEAD_DIM, "gdn_v": Hg * HEAD_DIM, "gdn_z": Hg * HEAD_DIM,
             "gate_conv": D, "gate_gdn": D, "gate_mla": D}
    off, o = {}, 0
    for name in _PACKED_ORDER:
        off[name] = o
        o += width[name]
    assert off["mla_cq"] % RQ == 0 and off["mla_ckv"] % RKV == 0
    return off, o


def _pack_w_in(w, C, Hg, RQ, RKV, D):
    names = ("conv_x", "conv_b", "conv_c", "gdn_q", "gdn_k", "gdn_v", "gdn_z", "gdn_small", "mla_cq", "mla_ckv",
             "mla_kpe", "gate_conv", "gate_gdn", "gate_mla")
    sizes = (C, C, C, Hg * HEAD_DIM, Hg * HEAD_DIM, Hg * HEAD_DIM, Hg * HEAD_DIM, 4 * Hg, RQ, RKV, ROPE_DIM, D, D, D)
    w = w.astype(BF16)
    seg, o = {}, 0
    for name, size in zip(names, sizes):
        seg[name] = w[..., o:o + size]
        o += size
    seg["gdn_small"] = jnp.pad(seg["gdn_small"], ((0, 0), (0, 0), (0, LANE - 4 * Hg)))
    seg["mla_kpe"] = jnp.concatenate([seg["mla_kpe"], _swap_rope(seg["mla_kpe"])], axis=-1)
    return jnp.concatenate([seg[n] for n in _PACKED_ORDER], axis=-1)


def _rope_tables(T):
    rows = T // GRID_W
    row = jnp.repeat(jnp.arange(rows), GRID_W)
    col = jnp.tile(jnp.arange(GRID_W), rows)
    n_freq = ROPE_DIM // 4
    inv_freq = 1.0 / (ROPE_THETA ** (jnp.arange(n_freq, dtype=F32) / n_freq))
    ang = jnp.concatenate([row[:, None] * inv_freq, col[:, None] * inv_freq], axis=-1)
    cos, sin = jnp.cos(ang), jnp.sin(ang)
    return jnp.concatenate([cos, cos], axis=1), jnp.concatenate([sin, sin], axis=1)


def kernel(x, c, ctx, c_ctx, w_ada, b_ada, norm1_g, w_in, conv_w, conv_out, gdn_conv_w, gdn_a_log, gdn_dt_bias,
           gdn_norm_g, gdn_out, mla_q_norm_g, mla_w_uq, mla_kv_norm_g, mla_w_ukv, mla_out, w_o, norm2_g, router_w,
           w_gate, w_up, w_down, final_g):
    B, T, D = x.shape
    TC = ctx.shape[1]
    L = w_ada.shape[0]
    C = conv_w.shape[-1]
    Hg = gdn_a_log.shape[-1]
    RQ = mla_q_norm_g.shape[-1]
    RKV = mla_kv_norm_g.shape[-1]
    Hm = mla_w_uq.shape[-1] // QK_DIM
    assert 4 * Hg <= LANE and T % GDN_CHUNK == 0 and TC % GDN_CHUNK == 0

    off, _ = _in_layout(C, Hg, RQ, RKV, D)

    R = -(-(B + 1) // 8) * 8
    cc = jnp.concatenate([c, c_ctx[None, :], jnp.zeros((R - B - 1, D), F32)], axis=0)
    mod = ada_modulation(cc, w_ada, b_ada)

    cos64, sin64 = _rope_tables(T)
    q_ct, q_st = jnp.tile(cos64, (1, Hm)), jnp.tile(sin64, (1, Hm))
    q_ct0, q_st0 = jnp.ones((TC, Hm * ROPE_DIM), F32), jnp.zeros((TC, Hm * ROPE_DIM), F32)
    k_tab = jnp.concatenate([cos64, sin64], 1)
    k_tab0 = jnp.concatenate([jnp.ones((TC, ROPE_DIM), F32), jnp.zeros((TC, ROPE_DIM), F32)], 1)

    h = x.reshape(B * T, D)
    hc = ctx.reshape(B * TC, D)
    zero_state = jnp.zeros((B, Hg, HEAD_DIM, HEAD_DIM), F32)

    w_in_p = _pack_w_in(w_in, C, Hg, RQ, RKV, D)
    wc_o, wg_o, wm_o, wo_b = (a.astype(BF16) for a in (conv_out, gdn_out, mla_out, w_o))
    router_t = jnp.swapaxes(router_w, 1, 2).astype(BF16)
    wgt, wup, wdn = w_gate.astype(BF16), w_up.astype(BF16), w_down.astype(BF16)

    for l in range(L):
        need_ctx = l < L - 1
        wq = mla_w_uq[l].reshape(RQ, Hm, QK_DIM)
        wq_pe = wq[..., HEAD_DIM:]
        wq_main = jnp.concatenate([wq[..., :HEAD_DIM].reshape(RQ, -1), wq_pe.reshape(RQ, -1)], 1).astype(BF16)
        wq_aux = _swap_rope(wq_pe).reshape(RQ, -1).astype(BF16)
        wkv = mla_w_ukv[l].reshape(RKV, Hm, 2 * HEAD_DIM)
        w_k = wkv[..., :HEAD_DIM].reshape(RKV, -1).astype(BF16)
        w_vt = wkv[..., HEAD_DIM:].transpose(1, 2, 0).astype(BF16)
        pad = jnp.zeros((LANE - 2 * Hg,), F32)
        alog_row = jnp.concatenate([gdn_a_log[l].reshape(-1), pad]).reshape(1, LANE)
        dtb_row = jnp.concatenate([gdn_dt_bias[l].reshape(-1), pad]).reshape(1, LANE)

        ml = mod[l, :B].reshape(B, 1, 6 * D)
        mc = mod[l, B:B + 1].reshape(1, 1, 6 * D)
        sh1, sc1, g1, sh2, sc2, g2 = (ml[..., i * D:(i + 1) * D] for i in range(6))
        csh1, csc1, cg1, csh2, csc2, cg2 = (mc[..., i * D:(i + 1) * D] for i in range(6))

        u_lat = in_projection(norm_modulate(h, norm1_g[l], sh1, sc1, T), w_in_p, l)
        u_ctx = in_projection(norm_modulate(hc, norm1_g[l], csh1, csc1, B * TC), w_in_p, l)

        ag_ctx, s_f, s_b = gdn_mixer(u_ctx, gdn_conv_w[l], alog_row, dtb_row, gdn_norm_g[l], zero_state, zero_state,
                                     B, TC, Hg, off)
        ag_lat, _, _ = gdn_mixer(u_lat, gdn_conv_w[l], alog_row, dtb_row, gdn_norm_g[l], s_f, s_b, B, T, Hg, off)

        k_ctx, v_ctx = mla_keys_values(u_ctx, mla_kv_norm_g[l], w_k, w_vt, k_tab0, B, TC, Hm, off)
        k_lat, v_lat = mla_keys_values(u_lat, mla_kv_norm_g[l], w_k, w_vt, k_tab, B, T, Hm, off)
        q_lat = mla_queries(u_lat, mla_q_norm_g[l], wq_main, wq_aux, q_ct, q_st, B, T, Hm, off)
        am_lat = attention(q_lat, [k_ctx, k_lat], [v_ctx, v_lat])

        ac_lat = conv_mixer_front(u_lat, conv_w[l], B, T, off)
        m_lat = merge_branches(ac_lat, ag_lat, am_lat, wc_o, wg_o, wm_o, l, u_lat, off)
        h, hn, lg = out_projection_router(m_lat, wo_b, l, h, g1, norm2_g[l], sh2, sc2, router_t, B, T, T)
        h = expert_choice_moe(h, hn, lg, g2, wgt, wup, wdn, l, B, T)

        if need_ctx:
            q_ctx = mla_queries(u_ctx, mla_q_norm_g[l], wq_main, wq_aux, q_ct0, q_st0, B, TC, Hm, off)
            am_ctx = attention(q_ctx, [k_ctx], [v_ctx])
            ac_ctx = conv_mixer_front(u_ctx, conv_w[l], B, TC, off)
            m_ctx = merge_branches(ac_ctx, ag_ctx, am_ctx, wc_o, wg_o, wm_o, l, u_ctx, off)
            hc, hn, lg = out_projection_router(m_ctx, wo_b, l, hc, cg1, norm2_g[l], csh2, csc2, router_t, B, TC,
                                               B * TC)
            hc = expert_choice_moe(hc, hn, lg, jnp.broadcast_to(cg2, (B, 1, D)), wgt, wup, wdn, l, B, TC)

    return final_norm(h, final_g).reshape(B, T, D)
```
